```python
import jax, jax.numpy as jnp
from jax import lax
import numpy as np

D_MODEL = 1024
BATCH = 8
SEQ = 4096
DEPTH = 2

CHUNK = 64
POOL_WINDOWS = (2, 4, 8, 16)
POOL_GROUP = 64
D_POOL = POOL_GROUP * len(POOL_WINDOWS)
N_HEADS = 8
HEAD_DIM = 64
D_ATTN = N_HEADS * HEAD_DIM
N_PREV_CHUNKS = 8
BAND = (N_PREV_CHUNKS + 1) * CHUNK
REL_CLIP = 128
N_REL = 2 * REL_CLIP + 1
D_CONV = 256
CONV_WIDTH = 31
D_FF = 4 * D_MODEL
N_BRANCH = 3
IN_SIZES = (D_POOL, D_ATTN, D_ATTN, D_ATTN, 2 * D_CONV, N_BRANCH * D_MODEL)
IN_SPLITS = tuple(int(v) for v in np.cumsum(IN_SIZES)[:-1])
D_IN = int(sum(IN_SIZES))
ALPHA = (2.0 * DEPTH) ** 0.25
BETA = (8.0 * DEPTH) ** -0.25
LN_EPS = 1e-5
NEG_INF = -1e30

kernel_name = "hybrid_chunk_causal_pool_attn_conv_block"


def layer_norm(x, g=None, b=None):
    xf = x.astype(jnp.float32)
    mu = jnp.mean(xf, axis=-1, keepdims=True)
    var = jnp.mean(jnp.square(xf - mu), axis=-1, keepdims=True)
    y = (xf - mu) * lax.rsqrt(var + LN_EPS)
    if g is not None:
        y = y * g.astype(jnp.float32) + b.astype(jnp.float32)
    return y.astype(x.dtype)


def pool_mixer(a, w_pool, pool_scale):
    B, S, _ = a.shape
    t = jnp.arange(S)
    outs = []
    for gi, w in enumerate(POOL_WINDOWS):
        xg = a[..., gi * POOL_GROUP:(gi + 1) * POOL_GROUP].astype(jnp.float32)
        cs = jnp.cumsum(xg, axis=1)
        cs_lag = jnp.pad(cs, ((0, 0), (w, 0), (0, 0)))[:, :S]
        count = jnp.minimum(t + 1, w).astype(jnp.float32)[None, :, None]
        outs.append((cs - cs_lag) / count - xg)
    p = jnp.stack(outs, axis=2).astype(a.dtype)
    p = jnp.einsum('bsgc,gcd->bsgd', p, w_pool).reshape(B, S, D_POOL)
    return p * pool_scale


def chunk_attention(q, k, v, rel_bias):
    B, S, _ = q.shape
    nc = S // CHUNK
    q = q.reshape(B, nc, CHUNK, N_HEADS, HEAD_DIM) * (HEAD_DIM ** -0.5)
    pad = ((0, 0), (N_PREV_CHUNKS * CHUNK, 0), (0, 0))
    kc = jnp.pad(k, pad).reshape(B, nc + N_PREV_CHUNKS, CHUNK, N_HEADS, HEAD_DIM)
    vc = jnp.pad(v, pad).reshape(B, nc + N_PREV_CHUNKS, CHUNK, N_HEADS, HEAD_DIM)
    band_idx = jnp.arange(nc)[:, None] + jnp.arange(N_PREV_CHUNKS + 1)[None, :]
    kb = kc[:, band_idx].reshape(B, nc, BAND, N_HEADS, HEAD_DIM)
    vb = vc[:, band_idx].reshape(B, nc, BAND, N_HEADS, HEAD_DIM)
    s = jnp.einsum('bnqhd,bnkhd->bnhqk', q, kb).astype(jnp.float32)
    qi = jnp.arange(CHUNK)[:, None]
    kj = jnp.arange(BAND)[None, :]
    rel = jnp.clip(N_PREV_CHUNKS * CHUNK + qi - kj, -REL_CLIP, REL_CLIP) + REL_CLIP
    bias = rel_bias[:, rel].astype(jnp.float32)
    key_pos = jnp.arange(nc)[:, None] * CHUNK + kj - N_PREV_CHUNKS * CHUNK
    valid = (key_pos >= 0)[None, :, None, None, :]
    s = jnp.where(valid, s + bias[None, None], NEG_INF)
    p = jax.nn.softmax(s, axis=-1).astype(vb.dtype)
    o = jnp.einsum('bnhqk,bnkhd->bnqhd', p, vb)
    return o.reshape(B, S, D_ATTN)


def conv_module(cin, conv_w, conv_b, ln_g, ln_b):
    h = cin[..., :D_CONV] * jax.nn.sigmoid(cin[..., D_CONV:])
    h = jnp.pad(h, ((0, 0), (CONV_WIDTH - 1, 0), (0, 0)))
    h = lax.conv_general_dilated(h, conv_w[:, None, :].astype(h.dtype), window_strides=(1,),
                                 padding='VALID', dimension_numbers=('NWC', 'WIO', 'NWC'),
                                 feature_group_count=D_CONV) + conv_b
    return jax.nn.silu(layer_norm(h, ln_g, ln_b))


def setup_inputs(seed: int = 0) -> dict:
    key = jax.random.key(seed)
    ks = jax.random.split(key, 26)
    L, D = DEPTH, D_MODEL

    def nrm(k, shape, scale):
        return jax.random.normal(k, shape, jnp.float32) * scale

    return {
        'x': nrm(ks[0], (BATCH, SEQ, D), 1.0),
        'c': nrm(ks[1], (BATCH, D), 1.0),
        'w_ada': nrm(ks[2], (L, D, 6 * D), 0.5 * D ** -0.5),
        'b_ada': nrm(ks[3], (L, 6 * D), 0.02),
        'w_in': nrm(ks[4], (L, D, D_IN), D ** -0.5),
        'b_gate': nrm(ks[5], (L, N_BRANCH * D), 0.1),
        'w_pool': nrm(ks[6], (L, len(POOL_WINDOWS), POOL_GROUP, POOL_GROUP), POOL_GROUP ** -0.5),
        'pool_scale': 1.0 + nrm(ks[7], (L, D_POOL), 0.1),
        'rel_bias': nrm(ks[8], (L, N_HEADS, N_REL), 0.1),
        'conv_w': nrm(ks[9], (L, CONV_WIDTH, D_CONV), CONV_WIDTH ** -0.5),
        'conv_b': nrm(ks[10], (L, D_CONV), 0.02),
        'conv_ln_g': 1.0 + nrm(ks[11], (L, D_CONV), 0.05),
        'conv_ln_b': nrm(ks[12], (L, D_CONV), 0.02),
        'w_br_pool': nrm(ks[13], (L, D_POOL, D), BETA * D_POOL ** -0.5),
        'w_br_attn': nrm(ks[14], (L, D_ATTN, D), BETA * D_ATTN ** -0.5),
        'w_br_conv': nrm(ks[15], (L, D_CONV, D), BETA * D_CONV ** -0.5),
        'w_o': nrm(ks[16], (L, D, D), BETA * D ** -0.5),
        'ln_mix_g': 1.0 + nrm(ks[17], (L, D), 0.05),
        'ln_mix_b': nrm(ks[18], (L, D), 0.02),
        'w_ff1': nrm(ks[19], (L, D, D_FF), D ** -0.5),
        'b_ff1': nrm(ks[20], (L, D_FF), 0.02),
        'w_ff2': nrm(ks[21], (L, D_FF, D), BETA * D_FF ** -0.5),
        'b_ff2': nrm(ks[22], (L, D), 0.02),
        'ln_ff_g': 1.0 + nrm(ks[23], (L, D), 0.05),
        'ln_ff_b': nrm(ks[24], (L, D), 0.02),
    }


def reference(x, c, w_ada, b_ada, w_in, b_gate, w_pool, pool_scale, rel_bias, conv_w, conv_b,
              conv_ln_g, conv_ln_b, w_br_pool, w_br_attn, w_br_conv, w_o, ln_mix_g, ln_mix_b,
              w_ff1, b_ff1, w_ff2, b_ff2, ln_ff_g, ln_ff_b):
    B, S, _ = x.shape
    c_act = jax.nn.silu(c)
    for l in range(DEPTH):
        mod = (c_act @ w_ada[l] + b_ada[l])[:, None, :]
        sh_m, sc_m, g_m, sh_f, sc_f, g_f = jnp.split(mod, 6, axis=-1)

        u = layer_norm(x) * (1 + sc_m) + sh_m
        z = u @ w_in[l]
        z_pool, z_q, z_k, z_v, z_conv, z_gate = jnp.split(z, IN_SPLITS, axis=-1)
        y_pool = pool_mixer(z_pool, w_pool[l], pool_scale[l]) @ w_br_pool[l]
        y_attn = chunk_attention(z_q, z_k, z_v, rel_bias[l]) @ w_br_attn[l]
        y_conv = conv_module(z_conv, conv_w[l], conv_b[l], conv_ln_g[l], conv_ln_b[l]) @ w_br_conv[l]
        gates = jax.nn.sigmoid(z_gate + b_gate[l]).reshape(B, S, N_BRANCH, D_MODEL)
        merged = gates[:, :, 0] * y_pool + gates[:, :, 1] * y_attn + gates[:, :, 2] * y_conv
        mix_out = merged @ w_o[l]
        x = layer_norm(ALPHA * x + g_m * mix_out, ln_mix_g[l], ln_mix_b[l])

        u = layer_norm(x) * (1 + sc_f) + sh_f
        h = jnp.square(jax.nn.relu(u @ w_ff1[l] + b_ff1[l]))
        ff_out = h @ w_ff2[l] + b_ff2[l]
        x = layer_norm(ALPHA * x + g_f * ff_out, ln_ff_g[l], ln_ff_b[l])
    return x
```

```python
import functools

import jax
import jax.numpy as jnp
import numpy as np
from jax import lax
from jax.experimental import pallas as pl
from jax.experimental.pallas import tpu as pltpu

D_MODEL = 1024
DEPTH = 2
CHUNK = 64
POOL_WINDOWS = (2, 4, 8, 16)
POOL_GROUP = 64
D_POOL = POOL_GROUP * len(POOL_WINDOWS)
N_HEADS = 8
HEAD_DIM = 64
D_ATTN = N_HEADS * HEAD_DIM
N_PREV_CHUNKS = 8
REL_CLIP = 128
D_CONV = 256
CONV_WIDTH = 31
D_FF = 4 * D_MODEL
N_BRANCH = 3
ALPHA = (2.0 * DEPTH) ** 0.25
LN_EPS = 1e-5
NEG_INF = -1e30

OFF_POOL = 0
OFF_Q = OFF_POOL + D_POOL
OFF_K = OFF_Q + D_ATTN
OFF_V = OFF_K + D_ATTN
OFF_CONV = OFF_V + D_ATTN
OFF_GATE = OFF_CONV + 2 * D_CONV
D_IN = OFF_GATE + N_BRANCH * D_MODEL

LANES = 128
MIX_TILE = 256
CHUNKS_PER_TILE = MIX_TILE // CHUNK
BAND_CHUNKS = N_PREV_CHUNKS + 2
BAND = BAND_CHUNKS * CHUNK
KV_HIST = (BAND_CHUNKS - 1) * CHUNK
CONV_HIST = 32
POOL_HIST = 16
HEAD_PAIRS = N_HEADS // 2
COL_BLOCK = 256
FFN_TILE = 512
FF_BLOCK = 512
ADA_BLOCK = 1536
MIX_VMEM_LIMIT = 52 * 1024 * 1024
FFN_VMEM_LIMIT = 48 * 1024 * 1024
ADA_VMEM_LIMIT = 32 * 1024 * 1024

_F32 = jnp.float32
_BF16 = jnp.bfloat16


def _dot(a, b):
    return jnp.dot(a, b, preferred_element_type=_F32)


def _normalize(x):
    mu = jnp.mean(x, axis=-1, keepdims=True)
    xc = x - mu
    var = jnp.mean(xc * xc, axis=-1, keepdims=True)
    return xc * lax.rsqrt(var + LN_EPS)


def _ada_kernel(c_ref, w_ref, b_ref, o_ref):
    c = c_ref[...]
    ca = (c * jax.nn.sigmoid(c)).astype(_BF16)
    o_ref[...] = _dot(ca, w_ref[...].astype(_BF16)) + b_ref[...]


def _ada_call(c, w_ada, b_ada):
    depth, d, n = w_ada.shape
    batch = c.shape[0]
    return pl.pallas_call(
        _ada_kernel,
        grid=(depth, n // ADA_BLOCK),
        in_specs=[
            pl.BlockSpec((batch, d), lambda l, j: (0, 0)),
            pl.BlockSpec((None, d, ADA_BLOCK), lambda l, j: (l, 0, j)),
            pl.BlockSpec((None, 1, ADA_BLOCK), lambda l, j: (l, 0, j)),
        ],
        out_specs=pl.BlockSpec((None, batch, ADA_BLOCK), lambda l, j: (l, 0, j)),
        out_shape=jax.ShapeDtypeStruct((depth, batch, n), _F32),
        compiler_params=pltpu.CompilerParams(
            dimension_semantics=("arbitrary", "arbitrary"), vmem_limit_bytes=ADA_VMEM_LIMIT),
        name="ada_mod",
    )(c, w_ada, b_ada.reshape(depth, 1, n))


def _mixer_kernel(x_ref, mod_ref, w_in_ref, b_gate_ref, w_pool_ref, pool_scale_ref, bias_ref,
                  conv_w_ref, conv_b_ref, conv_g_ref, conv_beta_ref, w_bp_ref, w_ba_ref, w_bc_ref,
                  w_o_ref, ln_g_ref, ln_b_ref, o_ref,
                  u_buf, qe_buf, qo_buf, k_buf, v_buf, attn_buf, pool_buf, conv_buf, merged_buf):
    T, D = MIX_TILE, D_MODEL
    i = pl.program_id(1)

    @pl.when(i == 0)
    def _():
        k_buf[0:KV_HIST, :] = jnp.zeros((KV_HIST, D_ATTN), _BF16)
        v_buf[0:KV_HIST, :] = jnp.zeros((KV_HIST, D_ATTN), _BF16)
        pool_buf[0:POOL_HIST, :] = jnp.zeros((POOL_HIST, D_POOL), _F32)
        conv_buf[0:CONV_HIST, :] = jnp.zeros((CONV_HIST, D_CONV), _F32)

    x = x_ref[...]
    sh_m = mod_ref[:, 0:D]
    sc_m = mod_ref[:, D:2 * D]
    g_m = mod_ref[:, 2 * D:3 * D]
    u_buf[...] = (_normalize(x) * (1.0 + sc_m) + sh_m).astype(_BF16)

    pool_buf[POOL_HIST:POOL_HIST + T, :] = _dot(u_buf[...], w_in_ref[:, OFF_POOL:OFF_POOL + D_POOL])

    q = _dot(u_buf[...], w_in_ref[:, OFF_Q:OFF_Q + D_ATTN]) * (HEAD_DIM ** -0.5)
    even_head = (lax.broadcasted_iota(jnp.int32, (T, D_ATTN), 1) & (LANES - 1)) < HEAD_DIM
    qe_buf[...] = jnp.where(even_head, q, 0.0).astype(_BF16)
    qo_buf[...] = jnp.where(even_head, 0.0, q).astype(_BF16)
    k_buf[KV_HIST:KV_HIST + T, :] = _dot(u_buf[...], w_in_ref[:, OFF_K:OFF_K + D_ATTN]).astype(_BF16)
    v_buf[KV_HIST:KV_HIST + T, :] = _dot(u_buf[...], w_in_ref[:, OFF_V:OFF_V + D_ATTN]).astype(_BF16)

    zc = _dot(u_buf[...], w_in_ref[:, OFF_CONV:OFF_CONV + 2 * D_CONV])
    conv_buf[CONV_HIST:CONV_HIST + T, :] = zc[:, 0:D_CONV] * jax.nn.sigmoid(zc[:, D_CONV:2 * D_CONV])

    t_abs = i * T + lax.broadcasted_iota(jnp.int32, (T, LANES), 0)
    upper_half = lax.broadcasted_iota(jnp.int32, (T, LANES), 1) >= POOL_GROUP
    pooled = []
    for col, (w_lo, w_hi) in enumerate(((POOL_WINDOWS[0], POOL_WINDOWS[1]), (POOL_WINDOWS[2], POOL_WINDOWS[3]))):
        cs = slice(col * LANES, (col + 1) * LANES)
        x0 = pool_buf[POOL_HIST:POOL_HIST + T, cs]
        acc = x0
        for s in range(1, w_hi):
            xs = pool_buf[POOL_HIST - s:POOL_HIST - s + T, cs]
            acc = acc + (xs if s < w_lo else jnp.where(upper_half, xs, 0.0))
        window = jnp.where(upper_half, w_hi, w_lo)
        count = jnp.minimum(t_abs + 1, window).astype(_F32)
        pooled.append(acc / count - x0)
    pooled = jnp.concatenate(pooled, axis=1).astype(_BF16)
    pool_feat = (_dot(pooled, w_pool_ref[...]) * pool_scale_ref[...]).astype(_BF16)

    lane = lax.broadcasted_iota(jnp.int32, (CHUNK, LANES), 1)
    key_idx = lax.broadcasted_iota(jnp.int32, (1, BAND), 1)
    for j in range(CHUNKS_PER_TILE):
        chunk = i * CHUNKS_PER_TILE + j
        first_valid = jnp.maximum(BAND_CHUNKS - 1 - chunk, 1) * CHUNK
        mask_add = jnp.where(key_idx >= first_valid, 0.0, NEG_INF)
        rows = slice(j * CHUNK, (j + 1) * CHUNK)
        band = slice(j * CHUNK, j * CHUNK + BAND)
        for p in range(HEAD_PAIRS):
            cs = slice(p * LANES, (p + 1) * LANES)
            qs = jnp.concatenate([qe_buf[rows, cs], qo_buf[rows, cs]], axis=0)
            s = lax.dot_general(qs, k_buf[band, cs], (((1,), (1,)), ((), ())),
                                preferred_element_type=_F32)
            s = s + bias_ref[p] + mask_add
            e = jnp.exp(s - jnp.max(s, axis=-1, keepdims=True))
            denom = jnp.sum(e, axis=-1, keepdims=True)
            pv = _dot(e.astype(_BF16), v_buf[band, cs]) * (1.0 / denom)
            attn_buf[rows, cs] = jnp.where(lane < HEAD_DIM, pv[0:CHUNK], pv[CHUNK:2 * CHUNK]).astype(_BF16)

    acc = conv_w_ref[0:1, :] * conv_buf[CONV_HIST - (CONV_WIDTH - 1):CONV_HIST - (CONV_WIDTH - 1) + T, :]
    for tap in range(1, CONV_WIDTH):
        start = CONV_HIST - (CONV_WIDTH - 1) + tap
        acc = acc + conv_w_ref[tap:tap + 1, :] * conv_buf[start:start + T, :]
    hc = _normalize(acc + conv_b_ref[...]) * conv_g_ref[...] + conv_beta_ref[...]
    conv_feat = (hc * jax.nn.sigmoid(hc)).astype(_BF16)

    for cch in range(KV_HIST // CHUNK):
        dst = slice(cch * CHUNK, (cch + 1) * CHUNK)
        src = slice(T + cch * CHUNK, T + (cch + 1) * CHUNK)
        k_buf[dst, :] = k_buf[src, :]
        v_buf[dst, :] = v_buf[src, :]
    pool_buf[0:POOL_HIST, :] = pool_buf[T:T + POOL_HIST, :]
    conv_buf[0:CONV_HIST, :] = conv_buf[T:T + CONV_HIST, :]

    for nb in range(D // COL_BLOCK):
        cs = slice(nb * COL_BLOCK, (nb + 1) * COL_BLOCK)
        merged = None
        feats = (pool_feat, attn_buf[...], conv_feat)
        weights = (w_bp_ref, w_ba_ref, w_bc_ref)
        for br in range(N_BRANCH):
            gs = slice(OFF_GATE + br * D + nb * COL_BLOCK, OFF_GATE + br * D + (nb + 1) * COL_BLOCK)
            bs = slice(br * D + nb * COL_BLOCK, br * D + (nb + 1) * COL_BLOCK)
            gate = jax.nn.sigmoid(_dot(u_buf[...], w_in_ref[:, gs]) + b_gate_ref[:, bs])
            term = gate * _dot(feats[br], weights[br][:, cs])
            merged = term if merged is None else merged + term
        merged_buf[:, cs] = merged.astype(_BF16)

    mix = _dot(merged_buf[...], w_o_ref[...])
    r = ALPHA * x + g_m * mix
    o_ref[...] = _normalize(r) * ln_g_ref[...] + ln_b_ref[...]


def _const_spec(shape):
    zeros = (0,) * len(shape)
    return pl.BlockSpec(shape, lambda b, i: zeros, pipeline_mode=pl.Buffered(1))


def _mixer_call(x, mod, w_in, b_gate, w_pool_bd, pool_scale, bias_pairs, conv_w, conv_b, conv_g,
                conv_beta, w_bp, w_ba, w_bc, w_o, ln_g, ln_b):
    B, S, D = x.shape
    T = MIX_TILE
    consts = (w_in, b_gate, w_pool_bd, pool_scale, bias_pairs, conv_w, conv_b, conv_g, conv_beta,
              w_bp, w_ba, w_bc, w_o, ln_g, ln_b)
    return pl.pallas_call(
        _mixer_kernel,
        grid=(B, S // T),
        in_specs=[pl.BlockSpec((None, T, D), lambda b, i: (b, i, 0)),
                  pl.BlockSpec((None, 1, mod.shape[-1]), lambda b, i: (b, 0, 0))]
                 + [_const_spec(a.shape) for a in consts],
        out_specs=pl.BlockSpec((None, T, D), lambda b, i: (b, i, 0)),
        out_shape=jax.ShapeDtypeStruct((B, S, D), _F32),
        scratch_shapes=[
            pltpu.VMEM((T, D), _BF16),
            pltpu.VMEM((T, D_ATTN), _BF16),
            pltpu.VMEM((T, D_ATTN), _BF16),
            pltpu.VMEM((KV_HIST + T, D_ATTN), _BF16),
            pltpu.VMEM((KV_HIST + T, D_ATTN), _BF16),
            pltpu.VMEM((T, D_ATTN), _BF16),
            pltpu.VMEM((POOL_HIST + T, D_POOL), _F32),
            pltpu.VMEM((CONV_HIST + T, D_CONV), _F32),
            pltpu.VMEM((T, D), _BF16),
        ],
        compiler_params=pltpu.CompilerParams(
            dimension_semantics=("arbitrary", "arbitrary"), vmem_limit_bytes=MIX_VMEM_LIMIT),
        name="token_mixer",
    )(x, mod.reshape(B, 1, -1), *consts)


def _ffn_kernel(x_ref, mod_ref, w1_ref, b1_ref, w2_ref, b2_ref, ln_g_ref, ln_b_ref, o_ref, u_buf):
    D = D_MODEL
    x = x_ref[...]
    sh_f = mod_ref[:, 3 * D:4 * D]
    sc_f = mod_ref[:, 4 * D:5 * D]
    g_f = mod_ref[:, 5 * D:6 * D]
    u_buf[...] = (_normalize(x) * (1.0 + sc_f) + sh_f).astype(_BF16)
    acc = None
    for c in range(D_FF // FF_BLOCK):
        cs = slice(c * FF_BLOCK, (c + 1) * FF_BLOCK)
        h = jnp.maximum(_dot(u_buf[...], w1_ref[:, cs]) + b1_ref[:, cs], 0.0)
        part = _dot((h * h).astype(_BF16), w2_ref[cs, :])
        acc = part if acc is None else acc + part
    r = ALPHA * x + g_f * (acc + b2_ref[...])
    o_ref[...] = _normalize(r) * ln_g_ref[...] + ln_b_ref[...]


def _ffn_call(x, mod, w1, b1, w2, b2, ln_g, ln_b):
    B, S, D = x.shape
    T = FFN_TILE
    consts = (w1, b1, w2, b2, ln_g, ln_b)
    return pl.pallas_call(
        _ffn_kernel,
        grid=(B, S // T),
        in_specs=[pl.BlockSpec((None, T, D), lambda b, i: (b, i, 0)),
                  pl.BlockSpec((None, 1, mod.shape[-1]), lambda b, i: (b, 0, 0))]
                 + [_const_spec(a.shape) for a in consts],
        out_specs=pl.BlockSpec((None, T, D), lambda b, i: (b, i, 0)),
        out_shape=jax.ShapeDtypeStruct((B, S, D), _F32),
        scratch_shapes=[pltpu.VMEM((T, D), _BF16)],
        compiler_params=pltpu.CompilerParams(
            dimension_semantics=("arbitrary", "arbitrary"), vmem_limit_bytes=FFN_VMEM_LIMIT),
        name="relu2_mlp",
    )(x, mod.reshape(B, 1, -1), *consts)


def _rel_index():
    qi = np.arange(CHUNK)[:, None]
    kj = np.arange(BAND)[None, :]
    return np.clip((BAND - CHUNK) + qi - kj, -REL_CLIP, REL_CLIP) + REL_CLIP


def _block_diag(w):
    groups, gin, gout = w.shape
    out = jnp.zeros((groups * gin, groups * gout), w.dtype)
    for g in range(groups):
        out = out.at[g * gin:(g + 1) * gin, g * gout:(g + 1) * gout].set(w[g])
    return out


def kernel(x, c, w_ada, b_ada, w_in, b_gate, w_pool, pool_scale, rel_bias, conv_w, conv_b, conv_ln_g, conv_ln_b, w_br_pool, w_br_attn, w_br_conv, w_o, ln_mix_g, ln_mix_b, w_ff1, b_ff1, w_ff2, b_ff2, ln_ff_g, ln_ff_b):
    mod = _ada_call(c, w_ada, b_ada)
    rel = _rel_index()
    row = lambda a: a.reshape(1, -1)
    for l in range(DEPTH):
        bias = rel_bias[l][:, rel]
        bias_pairs = bias.reshape(HEAD_PAIRS, 2 * CHUNK, BAND)
        x = _mixer_call(
            x, mod[l], w_in[l].astype(_BF16), row(b_gate[l]), _block_diag(w_pool[l]).astype(_BF16),
            row(pool_scale[l]), bias_pairs, conv_w[l], row(conv_b[l]), row(conv_ln_g[l]),
            row(conv_ln_b[l]), w_br_pool[l].astype(_BF16), w_br_attn[l].astype(_BF16),
            w_br_conv[l].astype(_BF16), w_o[l].astype(_BF16), row(ln_mix_g[l]), row(ln_mix_b[l]))
        x = _ffn_call(x, mod[l], w_ff1[l].astype(_BF16), row(b_ff1[l]), w_ff2[l].astype(_BF16),
                      row(b_ff2[l]), row(ln_ff_g[l]), row(ln_ff_b[l]))
    return x
```

```python
import jax
import jax.numpy as jnp
import numpy as np
from jax import lax
from jax.experimental import pallas as pl
from jax.experimental.pallas import tpu as pltpu

D_MODEL = 1024
DEPTH = 2
CHUNK = 64
POOL_WINDOWS = (2, 4, 8, 16)
POOL_GROUP = 64
D_POOL = POOL_GROUP * len(POOL_WINDOWS)
N_HEADS = 8
HEAD_DIM = 64
D_ATTN = N_HEADS * HEAD_DIM
N_PREV_CHUNKS = 8
REL_CLIP = 128
D_CONV = 256
CONV_WIDTH = 31
D_FF = 4 * D_MODEL
N_BRANCH = 3
ALPHA = (2.0 * DEPTH) ** 0.25
LN_EPS = 1e-5
NEG_INF = -1e30

OFF_POOL = 0
OFF_Q = OFF_POOL + D_POOL
OFF_K = OFF_Q + D_ATTN
OFF_V = OFF_K + D_ATTN
OFF_CONV = OFF_V + D_ATTN
OFF_GATE = OFF_CONV + 2 * D_CONV
D_IN = OFF_GATE + N_BRANCH * D_MODEL

LANES = 128
SUBLANES = 8
MIX_TILE = 256
CHUNKS_PER_TILE = MIX_TILE // CHUNK
BAND_CHUNKS = N_PREV_CHUNKS + 2
BAND = BAND_CHUNKS * CHUNK
KV_HIST = (BAND_CHUNKS - 1) * CHUNK
CONV_HIST = 32
POOL_HIST = 16
HEAD_PAIRS = N_HEADS // 2
SCORE_SLOTS = 2 * HEAD_PAIRS
COL_BLOCK = 256
FFN_TILE = 512
FF_BLOCK = 512
ADA_BLOCK = 1536
MIX_VMEM_LIMIT = 52 * 1024 * 1024
FFN_VMEM_LIMIT = 48 * 1024 * 1024
ADA_VMEM_LIMIT = 32 * 1024 * 1024

_F32 = jnp.float32
_BF16 = jnp.bfloat16


def _dot(a, b):
    return jnp.dot(a, b, preferred_element_type=_F32)


def _normalize(x):
    mu = jnp.mean(x, axis=-1, keepdims=True)
    xc = x - mu
    var = jnp.mean(xc * xc, axis=-1, keepdims=True)
    return xc * lax.rsqrt(var + LN_EPS)


def _ada_kernel(c_ref, w_ref, b_ref, o_ref):
    c = c_ref[...]
    ca = (c * jax.nn.sigmoid(c)).astype(_BF16)
    o_ref[...] = _dot(ca, w_ref[...].astype(_BF16)) + b_ref[...]


def _ada_call(c, w_ada, b_ada):
    depth, d, n = w_ada.shape
    batch = c.shape[0]
    return pl.pallas_call(
        _ada_kernel,
        grid=(depth, n // ADA_BLOCK),
        in_specs=[
            pl.BlockSpec((batch, d), lambda l, j: (0, 0)),
            pl.BlockSpec((None, d, ADA_BLOCK), lambda l, j: (l, 0, j)),
            pl.BlockSpec((None, 1, ADA_BLOCK), lambda l, j: (l, 0, j)),
        ],
        out_specs=pl.BlockSpec((None, batch, ADA_BLOCK), lambda l, j: (l, 0, j)),
        out_shape=jax.ShapeDtypeStruct((depth, batch, n), _F32),
        compiler_params=pltpu.CompilerParams(
            dimension_semantics=("arbitrary", "arbitrary"), vmem_limit_bytes=ADA_VMEM_LIMIT),
        name="ada_mod",
    )(c, w_ada, b_ada.reshape(depth, 1, n))


def _mixer_kernel(x_ref, mod_ref, w_in_ref, b_gate_ref, w_pool_ref, pool_scale_ref, bias_ref,
                  conv_w_ref, conv_b_ref, conv_g_ref, conv_beta_ref, w_bp_ref, w_ba_ref, w_bc_ref,
                  w_o_ref, ln_g_ref, ln_b_ref, o_ref,
                  u_buf, qe_buf, qo_buf, k_buf, v_buf, attn_buf, pool_buf, conv_buf, conv_shift,
                  score_buf, prob_buf, rdenom_buf, gate_buf, merged_buf):
    T, D = MIX_TILE, D_MODEL
    i = pl.program_id(1)

    @pl.when(i == 0)
    def _():
        k_buf[0:KV_HIST, :] = jnp.zeros((KV_HIST, D_ATTN), _BF16)
        v_buf[0:KV_HIST, :] = jnp.zeros((KV_HIST, D_ATTN), _BF16)
        pool_buf[0:POOL_HIST, :] = jnp.zeros((POOL_HIST, D_POOL), _F32)
        conv_buf[0:CONV_HIST, :] = jnp.zeros((CONV_HIST, D_CONV), _F32)

    x = x_ref[...]
    sh_m = mod_ref[:, 0:D]
    sc_m = mod_ref[:, D:2 * D]
    g_m = mod_ref[:, 2 * D:3 * D]
    u_buf[...] = (_normalize(x) * (1.0 + sc_m) + sh_m).astype(_BF16)

    def in_proj(off, width):
        return _dot(u_buf[...], w_in_ref[:, off:off + width])

    lane = lax.broadcasted_iota(jnp.int32, (CHUNK, LANES), 1)
    key_idx = lax.broadcasted_iota(jnp.int32, (1, BAND), 1)

    def scores(j):
        rows = slice(j * CHUNK, (j + 1) * CHUNK)
        band = slice(j * CHUNK, j * CHUNK + BAND)
        for p in range(HEAD_PAIRS):
            cs = slice(p * LANES, (p + 1) * LANES)
            qs = jnp.concatenate([qe_buf[rows, cs], qo_buf[rows, cs]], axis=0)
            score_buf[(j % 2) * HEAD_PAIRS + p] = lax.dot_general(
                qs, k_buf[band, cs], (((1,), (1,)), ((), ())), preferred_element_type=_F32)

    def softmax(j):
        chunk = i * CHUNKS_PER_TILE + j
        first_valid = jnp.maximum(BAND_CHUNKS - 1 - chunk, 1) * CHUNK
        mask_add = jnp.where(key_idx >= first_valid, 0.0, NEG_INF)
        for p in range(HEAD_PAIRS):
            slot = (j % 2) * HEAD_PAIRS + p
            s = score_buf[slot] + bias_ref[p] + mask_add
            e = jnp.exp(s - jnp.max(s, axis=-1, keepdims=True))
            rdenom_buf[slot] = 1.0 / jnp.sum(e, axis=-1, keepdims=True)
            prob_buf[slot] = e.astype(_BF16)

    def weighted_values(j):
        rows = slice(j * CHUNK, (j + 1) * CHUNK)
        band = slice(j * CHUNK, j * CHUNK + BAND)
        for p in range(HEAD_PAIRS):
            cs = slice(p * LANES, (p + 1) * LANES)
            slot = (j % 2) * HEAD_PAIRS + p
            pv = _dot(prob_buf[slot], v_buf[band, cs]) * rdenom_buf[slot]
            attn_buf[rows, cs] = jnp.where(lane < HEAD_DIM, pv[0:CHUNK], pv[CHUNK:2 * CHUNK]).astype(_BF16)

    def gates(first, count):
        for g in range(first, first + count):
            cs = slice(g * COL_BLOCK, (g + 1) * COL_BLOCK)
            gate_buf[:, cs] = jax.nn.sigmoid(in_proj(OFF_GATE + g * COL_BLOCK, COL_BLOCK) + b_gate_ref[:, cs])

    q = in_proj(OFF_Q, D_ATTN) * (HEAD_DIM ** -0.5)
    even_head = (lax.broadcasted_iota(jnp.int32, (T, D_ATTN), 1) & (LANES - 1)) < HEAD_DIM
    qe_buf[...] = jnp.where(even_head, q, 0.0).astype(_BF16)
    qo_buf[...] = jnp.where(even_head, 0.0, q).astype(_BF16)
    k_buf[KV_HIST:KV_HIST + T, :] = in_proj(OFF_K, D_ATTN).astype(_BF16)
    v_buf[KV_HIST:KV_HIST + T, :] = in_proj(OFF_V, D_ATTN).astype(_BF16)
    scores(0)

    pool_buf[POOL_HIST:POOL_HIST + T, :] = in_proj(OFF_POOL, D_POOL)
    zc = in_proj(OFF_CONV, 2 * D_CONV)
    conv_buf[CONV_HIST:CONV_HIST + T, :] = zc[:, 0:D_CONV] * jax.nn.sigmoid(zc[:, D_CONV:2 * D_CONV])
    softmax(0)

    t_abs = i * T + lax.broadcasted_iota(jnp.int32, (T, LANES), 0)
    upper_half = lax.broadcasted_iota(jnp.int32, (T, LANES), 1) >= POOL_GROUP
    pooled = []
    for col, (w_lo, w_hi) in enumerate(((POOL_WINDOWS[0], POOL_WINDOWS[1]), (POOL_WINDOWS[2], POOL_WINDOWS[3]))):
        cs = slice(col * LANES, (col + 1) * LANES)
        x0 = pool_buf[POOL_HIST:POOL_HIST + T, cs]
        acc = x0
        for s in range(1, w_hi):
            xs = pool_buf[POOL_HIST - s:POOL_HIST - s + T, cs]
            acc = acc + (xs if s < w_lo else jnp.where(upper_half, xs, 0.0))
        window = jnp.where(upper_half, w_hi, w_lo)
        count = jnp.minimum(t_abs + 1, window).astype(_F32)
        pooled.append(acc / count - x0)
    pooled = jnp.concatenate(pooled, axis=1).astype(_BF16)
    pool_feat = (_dot(pooled, w_pool_ref[...]) * pool_scale_ref[...]).astype(_BF16)

    gates_per_stage = (N_BRANCH * D // COL_BLOCK) // CHUNKS_PER_TILE
    for j in range(CHUNKS_PER_TILE):
        if j + 1 < CHUNKS_PER_TILE:
            scores(j + 1)
        weighted_values(j)
        gates(j * gates_per_stage, gates_per_stage)
        if j + 1 < CHUNKS_PER_TILE:
            softmax(j + 1)

    span = CONV_HIST + T - SUBLANES
    for r in range(1, SUBLANES):
        conv_shift[r - 1, 0:span, :] = conv_buf[r:r + span, :]
    acc = None
    for tap in range(CONV_WIDTH):
        start = CONV_HIST - (CONV_WIDTH - 1) + tap
        base, r = (start // SUBLANES) * SUBLANES, start % SUBLANES
        rows = conv_buf[base:base + T, :] if r == 0 else conv_shift[r - 1, base:base + T, :]
        term = conv_w_ref[tap:tap + 1, :] * rows
        acc = term if acc is None else acc + term
    hc = _normalize(acc + conv_b_ref[...]) * conv_g_ref[...] + conv_beta_ref[...]
    conv_feat = (hc * jax.nn.sigmoid(hc)).astype(_BF16)

    for cch in range(KV_HIST // CHUNK):
        dst = slice(cch * CHUNK, (cch + 1) * CHUNK)
        src = slice(T + cch * CHUNK, T + (cch + 1) * CHUNK)
        k_buf[dst, :] = k_buf[src, :]
        v_buf[dst, :] = v_buf[src, :]
    pool_buf[0:POOL_HIST, :] = pool_buf[T:T + POOL_HIST, :]
    conv_buf[0:CONV_HIST, :] = conv_buf[T:T + CONV_HIST, :]

    feats = (pool_feat, attn_buf[...], conv_feat)
    weights = (w_bp_ref, w_ba_ref, w_bc_ref)
    for nb in range(D // COL_BLOCK):
        cs = slice(nb * COL_BLOCK, (nb + 1) * COL_BLOCK)
        merged = None
        for br in range(N_BRANCH):
            gs = slice(br * D + nb * COL_BLOCK, br * D + (nb + 1) * COL_BLOCK)
            term = gate_buf[:, gs] * _dot(feats[br], weights[br][:, cs])
            merged = term if merged is None else merged + term
        merged_buf[:, cs] = merged.astype(_BF16)

    mix = _dot(merged_buf[...], w_o_ref[...])
    r = ALPHA * x + g_m * mix
    o_ref[...] = _normalize(r) * ln_g_ref[...] + ln_b_ref[...]


def _const_spec(shape):
    zeros = (0,) * len(shape)
    return pl.BlockSpec(shape, lambda b, i: zeros, pipeline_mode=pl.Buffered(1))


def _mixer_call(x, mod, w_in, b_gate, w_pool_bd, pool_scale, bias_pairs, conv_w, conv_b, conv_g,
                conv_beta, w_bp, w_ba, w_bc, w_o, ln_g, ln_b):
    B, S, D = x.shape
    T = MIX_TILE
    consts = (w_in, b_gate, w_pool_bd, pool_scale, bias_pairs, conv_w, conv_b, conv_g, conv_beta,
              w_bp, w_ba, w_bc, w_o, ln_g, ln_b)
    return pl.pallas_call(
        _mixer_kernel,
        grid=(B, S // T),
        in_specs=[pl.BlockSpec((None, T, D), lambda b, i: (b, i, 0)),
                  pl.BlockSpec((None, 1, mod.shape[-1]), lambda b, i: (b, 0, 0))]
                 + [_const_spec(a.shape) for a in consts],
        out_specs=pl.BlockSpec((None, T, D), lambda b, i: (b, i, 0)),
        out_shape=jax.ShapeDtypeStruct((B, S, D), _F32),
        scratch_shapes=[
            pltpu.VMEM((T, D), _BF16),
            pltpu.VMEM((T, D_ATTN), _BF16),
            pltpu.VMEM((T, D_ATTN), _BF16),
            pltpu.VMEM((KV_HIST + T, D_ATTN), _BF16),
            pltpu.VMEM((KV_HIST + T, D_ATTN), _BF16),
            pltpu.VMEM((T, D_ATTN), _BF16),
            pltpu.VMEM((POOL_HIST + T, D_POOL), _F32),
            pltpu.VMEM((CONV_HIST + T, D_CONV), _F32),
            pltpu.VMEM((SUBLANES - 1, CONV_HIST + T, D_CONV), _F32),
            pltpu.VMEM((SCORE_SLOTS, 2 * CHUNK, BAND), _F32),
            pltpu.VMEM((SCORE_SLOTS, 2 * CHUNK, BAND), _BF16),
            pltpu.VMEM((SCORE_SLOTS, 2 * CHUNK, 1), _F32),
            pltpu.VMEM((T, N_BRANCH * D), _F32),
            pltpu.VMEM((T, D), _BF16),
        ],
        compiler_params=pltpu.CompilerParams(
            dimension_semantics=("arbitrary", "arbitrary"), vmem_limit_bytes=MIX_VMEM_LIMIT),
        name="token_mixer",
    )(x, mod.reshape(B, 1, -1), *consts)


def _ffn_kernel(x_ref, mod_ref, w1_ref, b1_ref, w2_ref, b2_ref, ln_g_ref, ln_b_ref, o_ref, u_buf):
    D = D_MODEL
    x = x_ref[...]
    sh_f = mod_ref[:, 3 * D:4 * D]
    sc_f = mod_ref[:, 4 * D:5 * D]
    g_f = mod_ref[:, 5 * D:6 * D]
    u_buf[...] = (_normalize(x) * (1.0 + sc_f) + sh_f).astype(_BF16)
    acc = None
    for c in range(D_FF // FF_BLOCK):
        cs = slice(c * FF_BLOCK, (c + 1) * FF_BLOCK)
        h = jnp.maximum(_dot(u_buf[...], w1_ref[:, cs]) + b1_ref[:, cs], 0.0)
        part = _dot((h * h).astype(_BF16), w2_ref[cs, :])
        acc = part if acc is None else acc + part
    r = ALPHA * x + g_f * (acc + b2_ref[...])
    o_ref[...] = _normalize(r) * ln_g_ref[...] + ln_b_ref[...]


def _ffn_call(x, mod, w1, b1, w2, b2, ln_g, ln_b):
    B, S, D = x.shape
    T = FFN_TILE
    consts = (w1, b1, w2, b2, ln_g, ln_b)
    return pl.pallas_call(
        _ffn_kernel,
        grid=(B, S // T),
        in_specs=[pl.BlockSpec((None, T, D), lambda b, i: (b, i, 0)),
                  pl.BlockSpec((None, 1, mod.shape[-1]), lambda b, i: (b, 0, 0))]
                 + [_const_spec(a.shape) for a in consts],
        out_specs=pl.BlockSpec((None, T, D), lambda b, i: (b, i, 0)),
        out_shape=jax.ShapeDtypeStruct((B, S, D), _F32),
        scratch_shapes=[pltpu.VMEM((T, D), _BF16)],
        compiler_params=pltpu.CompilerParams(
            dimension_semantics=("arbitrary", "arbitrary"), vmem_limit_bytes=FFN_VMEM_LIMIT),
        name="relu2_mlp",
    )(x, mod.reshape(B, 1, -1), *consts)


def _bias_pairs(rel_bias_l):
    m = np.arange(BAND + CHUNK - 1)
    ext_idx = np.clip(BAND - 1 - m, -REL_CLIP, REL_CLIP) + REL_CLIP
    ext = rel_bias_l[:, ext_idx]
    bias = jnp.stack([ext[:, CHUNK - 1 - q:CHUNK - 1 - q + BAND] for q in range(CHUNK)], axis=1)
    return bias.reshape(HEAD_PAIRS, 2 * CHUNK, BAND)


def _block_diag(w):
    groups, gin, gout = w.shape
    out = jnp.zeros((groups * gin, groups * gout), w.dtype)
    for g in range(groups):
        out = out.at[g * gin:(g + 1) * gin, g * gout:(g + 1) * gout].set(w[g])
    return out


def kernel(x, c, w_ada, b_ada, w_in, b_gate, w_pool, pool_scale, rel_bias, conv_w, conv_b, conv_ln_g, conv_ln_b, w_br_pool, w_br_attn, w_br_conv, w_o, ln_mix_g, ln_mix_b, w_ff1, b_ff1, w_ff2, b_ff2, ln_ff_g, ln_ff_b):
    mod = _ada_call(c, w_ada, b_ada)
    row = lambda a: a.reshape(1, -1)
    for l in range(DEPTH):
        x = _mixer_call(
            x, mod[l], w_in[l].astype(_BF16), row(b_gate[l]), _block_diag(w_pool[l]).astype(_BF16),
            row(pool_scale[l]), _bias_pairs(rel_bias[l]), conv_w[l], row(conv_b[l]),
            row(conv_ln_g[l]), row(conv_ln_b[l]), w_br_pool[l].astype(_BF16),
            w_br_attn[l].astype(_BF16), w_br_conv[l].astype(_BF16), w_o[l].astype(_BF16),
            row(ln_mix_g[l]), row(ln_mix_b[l]))
        x = _ffn_call(x, mod[l], w_ff1[l].astype(_BF16), row(b_ff1[l]), w_ff2[l].astype(_BF16),
                      row(b_ff2[l]), row(ln_ff_g[l]), row(ln_ff_b[l]))
    return x
```

```python
import jax
import jax.numpy as jnp
import numpy as np
from jax import lax
from jax.experimental import pallas as pl
from jax.experimental.pallas import tpu as pltpu

D_MODEL = 1024
DEPTH = 2
CHUNK = 64
POOL_WINDOWS = (2, 4, 8, 16)
POOL_GROUP = 64
D_POOL = POOL_GROUP * len(POOL_WINDOWS)
N_HEADS = 8
HEAD_DIM = 64
D_ATTN = N_HEADS * HEAD_DIM
N_PREV_CHUNKS = 8
REL_CLIP = 128
D_CONV = 256
CONV_WIDTH = 31
D_FF = 4 * D_MODEL
N_BRANCH = 3
ALPHA = (2.0 * DEPTH) ** 0.25
LN_EPS = 1e-5
NEG_INF = -1e30
LOG2_E = 1.4426950408889634

OFF_POOL = 0
OFF_Q = OFF_POOL + D_POOL
OFF_K = OFF_Q + D_ATTN
OFF_V = OFF_K + D_ATTN
OFF_CONV = OFF_V + D_ATTN
OFF_GATE = OFF_CONV + 2 * D_CONV
D_IN = OFF_GATE + N_BRANCH * D_MODEL

LANES = 128
SUBLANES = 8
MIX_TILE = 512
CHUNKS_PER_TILE = MIX_TILE // CHUNK
BAND_CHUNKS = N_PREV_CHUNKS + 2
BAND = BAND_CHUNKS * CHUNK
KV_HIST = (BAND_CHUNKS - 1) * CHUNK
CONV_HIST = 32
POOL_HIST = 16
HEAD_PAIRS = N_HEADS // 2
SCORE_SLOTS = 2 * HEAD_PAIRS
SOFTMAX_ROWS = 32
COL_BLOCK = 256
FFN_TILE = 512
FF_BLOCK = 512
ADA_BLOCK = 1536
MIX_VMEM_LIMIT = 52 * 1024 * 1024
FFN_VMEM_LIMIT = 48 * 1024 * 1024
ADA_VMEM_LIMIT = 32 * 1024 * 1024

_F32 = jnp.float32
_BF16 = jnp.bfloat16


def _dot(a, b):
    return jnp.dot(a, b, preferred_element_type=_F32)


def _normalize(x):
    mu = jnp.mean(x, axis=-1, keepdims=True)
    xc = x - mu
    var = jnp.mean(xc * xc, axis=-1, keepdims=True)
    return xc * lax.rsqrt(var + LN_EPS)


def _ada_kernel(c_ref, w_ref, b_ref, o_ref):
    c = c_ref[...]
    ca = (c * jax.nn.sigmoid(c)).astype(_BF16)
    o_ref[...] = _dot(ca, w_ref[...].astype(_BF16)) + b_ref[...]


def _ada_call(c, w_ada, b_ada):
    depth, d, n = w_ada.shape
    batch = c.shape[0]
    return pl.pallas_call(
        _ada_kernel,
        grid=(depth, n // ADA_BLOCK),
        in_specs=[
            pl.BlockSpec((batch, d), lambda l, j: (0, 0)),
            pl.BlockSpec((None, d, ADA_BLOCK), lambda l, j: (l, 0, j)),
            pl.BlockSpec((None, 1, ADA_BLOCK), lambda l, j: (l, 0, j)),
        ],
        out_specs=pl.BlockSpec((None, batch, ADA_BLOCK), lambda l, j: (l, 0, j)),
        out_shape=jax.ShapeDtypeStruct((depth, batch, n), _F32),
        compiler_params=pltpu.CompilerParams(
            dimension_semantics=("arbitrary", "arbitrary"), vmem_limit_bytes=ADA_VMEM_LIMIT),
        name="ada_mod",
    )(c, w_ada, b_ada.reshape(depth, 1, n))


def _mixer_kernel(x_ref, mod_ref, w_in_ref, b_gate_ref, w_pool_ref, pool_scale_ref, bias_ref,
                  conv_w_ref, conv_b_ref, conv_g_ref, conv_beta_ref, w_bp_ref, w_ba_ref, w_bc_ref,
                  w_o_ref, ln_g_ref, ln_b_ref, o_ref,
                  u_buf, qe_buf, qo_buf, k_buf, v_buf, attn_buf, pool_buf, conv_buf, conv_shift,
                  score_buf, prob_buf, rdenom_buf, gate_buf, merged_buf):
    T, D = MIX_TILE, D_MODEL
    i = pl.program_id(1)

    @pl.when(i == 0)
    def _():
        k_buf[0:KV_HIST, :] = jnp.zeros((KV_HIST, D_ATTN), _BF16)
        v_buf[0:KV_HIST, :] = jnp.zeros((KV_HIST, D_ATTN), _BF16)
        pool_buf[0:POOL_HIST, :] = jnp.zeros((POOL_HIST, D_POOL), _F32)
        conv_buf[0:CONV_HIST, :] = jnp.zeros((CONV_HIST, D_CONV), _F32)

    x = x_ref[...]
    sh_m = mod_ref[:, 0:D]
    sc_m = mod_ref[:, D:2 * D]
    g_m = mod_ref[:, 2 * D:3 * D]
    u_buf[...] = (_normalize(x) * (1.0 + sc_m) + sh_m).astype(_BF16)

    def in_proj(off, width):
        return _dot(u_buf[...], w_in_ref[:, off:off + width])

    lane = lax.broadcasted_iota(jnp.int32, (CHUNK, LANES), 1)
    key_idx = lax.broadcasted_iota(jnp.int32, (1, BAND), 1)
    dyn0 = jnp.minimum(i, 0)

    def scores(j):
        rows = slice(j * CHUNK, (j + 1) * CHUNK)
        band = slice(j * CHUNK, j * CHUNK + BAND)
        for p in range(HEAD_PAIRS):
            cs = slice(p * LANES, (p + 1) * LANES)
            qs = jnp.concatenate([qe_buf[rows, cs], qo_buf[rows, cs]], axis=0)
            score_buf[(j % 2) * HEAD_PAIRS + p + dyn0] = lax.dot_general(
                qs, k_buf[band, cs], (((1,), (1,)), ((), ())), preferred_element_type=_F32)

    def softmax(j):
        chunk = i * CHUNKS_PER_TILE + j
        first_valid = jnp.maximum(BAND_CHUNKS - 1 - chunk, 1) * CHUNK
        mask_add = jnp.where(key_idx >= first_valid, 0.0, NEG_INF)
        for p in range(HEAD_PAIRS):
            slot = (j % 2) * HEAD_PAIRS + p + dyn0
            for rb in range(2 * CHUNK // SOFTMAX_ROWS):
                rs = slice(rb * SOFTMAX_ROWS, (rb + 1) * SOFTMAX_ROWS)
                s = score_buf[slot, rs, :] + bias_ref[p, rs, :] + mask_add
                e = jnp.exp2(s - jnp.max(s, axis=-1, keepdims=True))
                rdenom_buf[slot, rs, :] = 1.0 / jnp.sum(e, axis=-1, keepdims=True)
                prob_buf[slot, rs, :] = e.astype(_BF16)

    def weighted_values(j):
        rows = slice(j * CHUNK, (j + 1) * CHUNK)
        band = slice(j * CHUNK, j * CHUNK + BAND)
        for p in range(HEAD_PAIRS):
            cs = slice(p * LANES, (p + 1) * LANES)
            slot = (j % 2) * HEAD_PAIRS + p + dyn0
            pv = _dot(prob_buf[slot], v_buf[band, cs]) * rdenom_buf[slot]
            attn_buf[rows, cs] = jnp.where(lane < HEAD_DIM, pv[0:CHUNK], pv[CHUNK:2 * CHUNK]).astype(_BF16)

    def gates(first, count):
        for g in range(first, first + count):
            cs = slice(g * COL_BLOCK, (g + 1) * COL_BLOCK)
            gate_buf[:, cs] = jax.nn.sigmoid(in_proj(OFF_GATE + g * COL_BLOCK, COL_BLOCK) + b_gate_ref[:, cs])

    q = in_proj(OFF_Q, D_ATTN) * (HEAD_DIM ** -0.5 * LOG2_E)
    even_head = (lax.broadcasted_iota(jnp.int32, (T, D_ATTN), 1) & (LANES - 1)) < HEAD_DIM
    qe_buf[...] = jnp.where(even_head, q, 0.0).astype(_BF16)
    qo_buf[...] = jnp.where(even_head, 0.0, q).astype(_BF16)
    k_buf[KV_HIST:KV_HIST + T, :] = in_proj(OFF_K, D_ATTN).astype(_BF16)
    v_buf[KV_HIST:KV_HIST + T, :] = in_proj(OFF_V, D_ATTN).astype(_BF16)
    scores(0)

    pool_buf[POOL_HIST:POOL_HIST + T, :] = in_proj(OFF_POOL, D_POOL)
    zc = in_proj(OFF_CONV, 2 * D_CONV)
    conv_buf[CONV_HIST:CONV_HIST + T, :] = zc[:, 0:D_CONV] * jax.nn.sigmoid(zc[:, D_CONV:2 * D_CONV])
    softmax(0)

    t_abs = i * T + lax.broadcasted_iota(jnp.int32, (T, LANES), 0)
    upper_half = lax.broadcasted_iota(jnp.int32, (T, LANES), 1) >= POOL_GROUP
    pooled = []
    for col, (w_lo, w_hi) in enumerate(((POOL_WINDOWS[0], POOL_WINDOWS[1]), (POOL_WINDOWS[2], POOL_WINDOWS[3]))):
        cs = slice(col * LANES, (col + 1) * LANES)
        x0 = pool_buf[POOL_HIST:POOL_HIST + T, cs]
        acc = x0
        for s in range(1, w_hi):
            xs = pool_buf[POOL_HIST - s:POOL_HIST - s + T, cs]
            acc = acc + (xs if s < w_lo else jnp.where(upper_half, xs, 0.0))
        window = jnp.where(upper_half, w_hi, w_lo)
        count = jnp.minimum(t_abs + 1, window).astype(_F32)
        pooled.append(acc / count - x0)
    pooled = jnp.concatenate(pooled, axis=1).astype(_BF16)
    pool_feat = (_dot(pooled, w_pool_ref[...]) * pool_scale_ref[...]).astype(_BF16)

    gate_blocks = N_BRANCH * D // COL_BLOCK
    for j in range(CHUNKS_PER_TILE):
        if j + 1 < CHUNKS_PER_TILE:
            scores(j + 1)
        weighted_values(j)
        first = j * gate_blocks // CHUNKS_PER_TILE
        gates(first, (j + 1) * gate_blocks // CHUNKS_PER_TILE - first)
        if j + 1 < CHUNKS_PER_TILE:
            softmax(j + 1)

    span = CONV_HIST + T - SUBLANES
    for r in range(1, SUBLANES):
        conv_shift[r - 1, 0:span, :] = conv_buf[r:r + span, :]
    acc = None
    for tap in range(CONV_WIDTH):
        start = CONV_HIST - (CONV_WIDTH - 1) + tap
        base, r = (start // SUBLANES) * SUBLANES, start % SUBLANES
        rows = conv_buf[base:base + T, :] if r == 0 else conv_shift[r - 1, base:base + T, :]
        term = conv_w_ref[tap:tap + 1, :] * rows
        acc = term if acc is None else acc + term
    hc = _normalize(acc + conv_b_ref[...]) * conv_g_ref[...] + conv_beta_ref[...]
    conv_feat = (hc * jax.nn.sigmoid(hc)).astype(_BF16)

    for cch in range(KV_HIST // CHUNK):
        dst = slice(cch * CHUNK, (cch + 1) * CHUNK)
        src = slice(T + cch * CHUNK, T + (cch + 1) * CHUNK)
        k_buf[dst, :] = k_buf[src, :]
        v_buf[dst, :] = v_buf[src, :]
    pool_buf[0:POOL_HIST, :] = pool_buf[T:T + POOL_HIST, :]
    conv_buf[0:CONV_HIST, :] = conv_buf[T:T + CONV_HIST, :]

    feats = (pool_feat, attn_buf[...], conv_feat)
    weights = (w_bp_ref, w_ba_ref, w_bc_ref)
    for nb in range(D // COL_BLOCK):
        cs = slice(nb * COL_BLOCK, (nb + 1) * COL_BLOCK)
        merged = None
        for br in range(N_BRANCH):
            gs = slice(br * D + nb * COL_BLOCK, br * D + (nb + 1) * COL_BLOCK)
            term = gate_buf[:, gs] * _dot(feats[br], weights[br][:, cs])
            merged = term if merged is None else merged + term
        merged_buf[:, cs] = merged.astype(_BF16)

    mix = _dot(merged_buf[...], w_o_ref[...])
    r = ALPHA * x + g_m * mix
    o_ref[...] = _normalize(r) * ln_g_ref[...] + ln_b_ref[...]


def _layer_spec(stacked, layer):
    index = (layer,) + (0,) * (stacked.ndim - 1)
    return pl.BlockSpec((None,) + stacked.shape[1:], lambda b, i: index, pipeline_mode=pl.Buffered(1))


def _mod_spec(mod, layer):
    return pl.BlockSpec((None, None, 1, mod.shape[-1]), lambda b, i: (layer, b, 0, 0))


def _mixer_call(layer, x, mod, w_in, b_gate, w_pool_bd, pool_scale, bias_pairs, conv_w, conv_b,
                conv_g, conv_beta, w_bp, w_ba, w_bc, w_o, ln_g, ln_b):
    B, S, D = x.shape
    T = MIX_TILE
    consts = (w_in, b_gate, w_pool_bd, pool_scale, bias_pairs, conv_w, conv_b, conv_g, conv_beta,
              w_bp, w_ba, w_bc, w_o, ln_g, ln_b)
    return pl.pallas_call(
        _mixer_kernel,
        grid=(B, S // T),
        in_specs=[pl.BlockSpec((None, T, D), lambda b, i: (b, i, 0)), _mod_spec(mod, layer)]
                 + [_layer_spec(a, layer) for a in consts],
        out_specs=pl.BlockSpec((None, T, D), lambda b, i: (b, i, 0)),
        out_shape=jax.ShapeDtypeStruct((B, S, D), _F32),
        scratch_shapes=[
            pltpu.VMEM((T, D), _BF16),
            pltpu.VMEM((T, D_ATTN), _BF16),
            pltpu.VMEM((T, D_ATTN), _BF16),
            pltpu.VMEM((KV_HIST + T, D_ATTN), _BF16),
            pltpu.VMEM((KV_HIST + T, D_ATTN), _BF16),
            pltpu.VMEM((T, D_ATTN), _BF16),
            pltpu.VMEM((POOL_HIST + T, D_POOL), _F32),
            pltpu.VMEM((CONV_HIST + T, D_CONV), _F32),
            pltpu.VMEM((SUBLANES - 1, CONV_HIST + T, D_CONV), _F32),
            pltpu.VMEM((SCORE_SLOTS, 2 * CHUNK, BAND), _F32),
            pltpu.VMEM((SCORE_SLOTS, 2 * CHUNK, BAND), _BF16),
            pltpu.VMEM((SCORE_SLOTS, 2 * CHUNK, 1), _F32),
            pltpu.VMEM((T, N_BRANCH * D), _F32),
            pltpu.VMEM((T, D), _BF16),
        ],
        compiler_params=pltpu.CompilerParams(
            dimension_semantics=("arbitrary", "arbitrary"), vmem_limit_bytes=MIX_VMEM_LIMIT),
        name="token_mixer",
    )(x, mod, *consts)


def _ffn_kernel(x_ref, mod_ref, w1_ref, b1_ref, w2_ref, b2_ref, ln_g_ref, ln_b_ref, o_ref, u_buf):
    D = D_MODEL
    x = x_ref[...]
    sh_f = mod_ref[:, 3 * D:4 * D]
    sc_f = mod_ref[:, 4 * D:5 * D]
    g_f = mod_ref[:, 5 * D:6 * D]
    u_buf[...] = (_normalize(x) * (1.0 + sc_f) + sh_f).astype(_BF16)
    acc = None
    for c in range(D_FF // FF_BLOCK):
        cs = slice(c * FF_BLOCK, (c + 1) * FF_BLOCK)
        h = jnp.maximum(_dot(u_buf[...], w1_ref[:, cs]) + b1_ref[:, cs], 0.0)
        part = _dot((h * h).astype(_BF16), w2_ref[cs, :])
        acc = part if acc is None else acc + part
    r = ALPHA * x + g_f * (acc + b2_ref[...])
    o_ref[...] = _normalize(r) * ln_g_ref[...] + ln_b_ref[...]


def _ffn_call(layer, x, mod, w1, b1, w2, b2, ln_g, ln_b):
    B, S, D = x.shape
    T = FFN_TILE
    consts = (w1, b1, w2, b2, ln_g, ln_b)
    return pl.pallas_call(
        _ffn_kernel,
        grid=(B, S // T),
        in_specs=[pl.BlockSpec((None, T, D), lambda b, i: (b, i, 0)), _mod_spec(mod, layer)]
                 + [_layer_spec(a, layer) for a in consts],
        out_specs=pl.BlockSpec((None, T, D), lambda b, i: (b, i, 0)),
        out_shape=jax.ShapeDtypeStruct((B, S, D), _F32),
        scratch_shapes=[pltpu.VMEM((T, D), _BF16)],
        compiler_params=pltpu.CompilerParams(
            dimension_semantics=("arbitrary", "arbitrary"), vmem_limit_bytes=FFN_VMEM_LIMIT),
        name="relu2_mlp",
    )(x, mod, *consts)


def _bias_pairs(rel_bias):
    period = BAND + CHUNK - 1
    m = np.arange(period)
    ext_idx = np.clip(BAND - 1 - m, -REL_CLIP, REL_CLIP) + REL_CLIP
    ext = rel_bias[:, :, ext_idx] * LOG2_E
    lead = ext.shape[:-1]
    hankel = jnp.tile(ext, (1, 1, CHUNK + 1))[..., :CHUNK * (period + 1)]
    hankel = hankel.reshape(*lead, CHUNK, period + 1)[..., :BAND]
    bias = hankel[..., ::-1, :]
    return bias.reshape(lead[0], HEAD_PAIRS, 2 * CHUNK, BAND)


def _block_diag(w):
    depth, groups, gin, gout = w.shape
    eye = jnp.eye(groups, dtype=w.dtype)[None, :, None, :, None]
    return (w[:, :, :, None, :] * eye).reshape(depth, groups * gin, groups * gout)


def kernel(x, c, w_ada, b_ada, w_in, b_gate, w_pool, pool_scale, rel_bias, conv_w, conv_b, conv_ln_g, conv_ln_b, w_br_pool, w_br_attn, w_br_conv, w_o, ln_mix_g, ln_mix_b, w_ff1, b_ff1, w_ff2, b_ff2, ln_ff_g, ln_ff_b):
    depth = w_in.shape[0]
    bf16 = lambda a: a.astype(_BF16)
    rows = lambda a: a.reshape(depth, 1, -1)
    mod = _ada_call(c, w_ada, b_ada)[:, :, None, :]
    mixer_params = (bf16(w_in), rows(b_gate), bf16(_block_diag(w_pool)), rows(pool_scale),
                    _bias_pairs(rel_bias), conv_w, rows(conv_b), rows(conv_ln_g), rows(conv_ln_b),
                    bf16(w_br_pool), bf16(w_br_attn), bf16(w_br_conv), bf16(w_o),
                    rows(ln_mix_g), rows(ln_mix_b))
    ffn_params = (bf16(w_ff1), rows(b_ff1), bf16(w_ff2), rows(b_ff2), rows(ln_ff_g), rows(ln_ff_b))
    for layer in range(depth):
        x = _mixer_call(layer, x, mod, *mixer_params)
        x = _ffn_call(layer, x, mod, *ffn_params)
    return x
```

```python
import functools

import jax
import jax.numpy as jnp
import numpy as np
from jax import lax
from jax.experimental import pallas as pl
from jax.experimental.pallas import tpu as pltpu

D_MODEL = 1024
DEPTH = 2
CHUNK = 64
POOL_WINDOWS = (2, 4, 8, 16)
POOL_GROUP = 64
D_POOL = POOL_GROUP * len(POOL_WINDOWS)
N_HEADS = 8
HEAD_DIM = 64
D_ATTN = N_HEADS * HEAD_DIM
N_PREV_CHUNKS = 8
REL_CLIP = 128
D_CONV = 256
CONV_WIDTH = 31
D_FF = 4 * D_MODEL
N_BRANCH = 3
ALPHA = (2.0 * DEPTH) ** 0.25
LN_EPS = 1e-5
NEG_INF = -1e30
LOG2_E = 1.4426950408889634

OFF_POOL = 0
OFF_Q = OFF_POOL + D_POOL
OFF_K = OFF_Q + D_ATTN
OFF_V = OFF_K + D_ATTN
OFF_CONV = OFF_V + D_ATTN
OFF_GATE = OFF_CONV + 2 * D_CONV
D_IN = OFF_GATE + N_BRANCH * D_MODEL

LANES = 128
SUBLANES = 8
MIX_TILE = 512
CHUNKS_PER_TILE = MIX_TILE // CHUNK
BAND_CHUNKS = N_PREV_CHUNKS + 2
BAND = BAND_CHUNKS * CHUNK
KV_HIST = (BAND_CHUNKS - 1) * CHUNK
CONV_HIST = 32
POOL_HIST = 16
HEAD_PAIRS = N_HEADS // 2
SCORE_SLOTS = 2 * HEAD_PAIRS
SOFTMAX_ROWS = 32
COL_BLOCK = 256
FFN_TILE = 512
FF_BLOCK = 512
ADA_BLOCK = 1536
MIX_VMEM_LIMIT = 52 * 1024 * 1024
FFN_VMEM_LIMIT = 48 * 1024 * 1024
ADA_VMEM_LIMIT = 32 * 1024 * 1024

_F32 = jnp.float32
_BF16 = jnp.bfloat16


def _dot(a, b):
    return jnp.dot(a, b, preferred_element_type=_F32)


def _normalize(x):
    mu = jnp.mean(x, axis=-1, keepdims=True)
    xc = x - mu
    var = jnp.mean(xc * xc, axis=-1, keepdims=True)
    return xc * lax.rsqrt(var + LN_EPS)


def _zero_token(v):
    rows, cols = v.shape
    folded = v.reshape(rows // SUBLANES, SUBLANES, cols).sum(axis=0)
    folded = sum(folded[:, k * LANES:(k + 1) * LANES] for k in range(cols // LANES))
    bits = lax.bitcast_convert_type(folded, jnp.uint32)
    half = jnp.uint32(16)
    zero = lax.shift_right_logical(lax.shift_right_logical(bits, half), half)
    return lax.bitcast_convert_type(zero, _F32)[0:1, :]


def _ada_kernel(c_ref, w_ref, b_ref, o_ref):
    c = c_ref[...]
    ca = (c * jax.nn.sigmoid(c)).astype(_BF16)
    o_ref[...] = _dot(ca, w_ref[...].astype(_BF16)) + b_ref[...]


def _ada_call(c, w_ada, b_ada):
    depth, d, n = w_ada.shape
    batch = c.shape[0]
    return pl.pallas_call(
        _ada_kernel,
        grid=(depth, n // ADA_BLOCK),
        in_specs=[
            pl.BlockSpec((batch, d), lambda l, j: (0, 0)),
            pl.BlockSpec((None, d, ADA_BLOCK), lambda l, j: (l, 0, j)),
            pl.BlockSpec((None, 1, ADA_BLOCK), lambda l, j: (l, 0, j)),
        ],
        out_specs=pl.BlockSpec((None, batch, ADA_BLOCK), lambda l, j: (l, 0, j)),
        out_shape=jax.ShapeDtypeStruct((depth, batch, n), _F32),
        compiler_params=pltpu.CompilerParams(
            dimension_semantics=("arbitrary", "arbitrary"), vmem_limit_bytes=ADA_VMEM_LIMIT),
        name="ada_mod",
    )(c, w_ada, b_ada.reshape(depth, 1, n))


def _mixer_kernel(x_ref, mod_ref, w_in_ref, b_gate_ref, w_pool_ref, pool_scale_ref, bias_ref,
                  conv_w_ref, conv_b_ref, conv_g_ref, conv_beta_ref, w_bp_ref, w_ba_ref, w_bc_ref,
                  w_o_ref, ln_g_ref, ln_b_ref, o_ref,
                  u_buf, qe_buf, qo_buf, k_buf, v_buf, attn_buf, pool_buf, conv_buf, conv_shift,
                  score_buf, prob_buf, rdenom_buf, gate_buf, merged_buf):
    T, D = MIX_TILE, D_MODEL
    i = pl.program_id(1)

    @pl.when(i == 0)
    def _():
        k_buf[0:KV_HIST, :] = jnp.zeros((KV_HIST, D_ATTN), _BF16)
        v_buf[0:KV_HIST, :] = jnp.zeros((KV_HIST, D_ATTN), _BF16)
        pool_buf[0:POOL_HIST, :] = jnp.zeros((POOL_HIST, D_POOL), _F32)
        conv_buf[0:CONV_HIST, :] = jnp.zeros((CONV_HIST, D_CONV), _F32)

    x = x_ref[...]
    sh_m = mod_ref[:, 0:D]
    sc_m = mod_ref[:, D:2 * D]
    g_m = mod_ref[:, 2 * D:3 * D]
    u_buf[...] = (_normalize(x) * (1.0 + sc_m) + sh_m).astype(_BF16)

    def in_proj(off, width):
        return _dot(u_buf[...], w_in_ref[:, off:off + width])

    lane = lax.broadcasted_iota(jnp.int32, (CHUNK, LANES), 1)
    key_idx = lax.broadcasted_iota(jnp.int32, (1, BAND), 1)
    dyn0 = jnp.minimum(i, 0)

    def scores(j):
        rows = slice(j * CHUNK, (j + 1) * CHUNK)
        band = slice(j * CHUNK, j * CHUNK + BAND)
        for p in range(HEAD_PAIRS):
            cs = slice(p * LANES, (p + 1) * LANES)
            qs = jnp.concatenate([qe_buf[rows, cs], qo_buf[rows, cs]], axis=0)
            score_buf[(j % 2) * HEAD_PAIRS + p + dyn0] = lax.dot_general(
                qs, k_buf[band, cs], (((1,), (1,)), ((), ())), preferred_element_type=_F32)

    def softmax(j):
        chunk = i * CHUNKS_PER_TILE + j
        first_valid = jnp.maximum(BAND_CHUNKS - 1 - chunk, 1) * CHUNK
        mask_add = jnp.where(key_idx >= first_valid, 0.0, NEG_INF)
        for p in range(HEAD_PAIRS):
            slot = (j % 2) * HEAD_PAIRS + p + dyn0
            for rb in range(2 * CHUNK // SOFTMAX_ROWS):
                rs = slice(rb * SOFTMAX_ROWS, (rb + 1) * SOFTMAX_ROWS)
                s = score_buf[slot, rs, :] + bias_ref[p, rs, :] + mask_add
                e = jnp.exp2(s - jnp.max(s, axis=-1, keepdims=True))
                rdenom_buf[slot, rs, :] = 1.0 / jnp.sum(e, axis=-1, keepdims=True)
                prob_buf[slot, rs, :] = e.astype(_BF16)

    def weighted_values(j):
        rows = slice(j * CHUNK, (j + 1) * CHUNK)
        band = slice(j * CHUNK, j * CHUNK + BAND)
        for p in range(HEAD_PAIRS):
            cs = slice(p * LANES, (p + 1) * LANES)
            slot = (j % 2) * HEAD_PAIRS + p + dyn0
            pv = _dot(prob_buf[slot], v_buf[band, cs]) * rdenom_buf[slot]
            attn_buf[rows, cs] = jnp.where(lane < HEAD_DIM, pv[0:CHUNK], pv[CHUNK:2 * CHUNK]).astype(_BF16)

    def gates(first, count):
        for g in range(first, first + count):
            cs = slice(g * COL_BLOCK, (g + 1) * COL_BLOCK)
            gate_buf[:, cs] = jax.nn.sigmoid(in_proj(OFF_GATE + g * COL_BLOCK, COL_BLOCK) + b_gate_ref[:, cs])

    q = in_proj(OFF_Q, D_ATTN) * (HEAD_DIM ** -0.5 * LOG2_E)
    even_head = (lax.broadcasted_iota(jnp.int32, (T, D_ATTN), 1) & (LANES - 1)) < HEAD_DIM
    qe_buf[...] = jnp.where(even_head, q, 0.0).astype(_BF16)
    qo_buf[...] = jnp.where(even_head, 0.0, q).astype(_BF16)
    k_buf[KV_HIST:KV_HIST + T, :] = in_proj(OFF_K, D_ATTN).astype(_BF16)
    v_buf[KV_HIST:KV_HIST + T, :] = in_proj(OFF_V, D_ATTN).astype(_BF16)
    scores(0)

    pool_buf[POOL_HIST:POOL_HIST + T, :] = in_proj(OFF_POOL, D_POOL)
    zc = in_proj(OFF_CONV, 2 * D_CONV)
    conv_buf[CONV_HIST:CONV_HIST + T, :] = zc[:, 0:D_CONV] * jax.nn.sigmoid(zc[:, D_CONV:2 * D_CONV])
    softmax(0)

    t_abs = i * T + lax.broadcasted_iota(jnp.int32, (T, LANES), 0)
    upper_half = lax.broadcasted_iota(jnp.int32, (T, LANES), 1) >= POOL_GROUP
    pooled = []
    for col, (w_lo, w_hi) in enumerate(((POOL_WINDOWS[0], POOL_WINDOWS[1]), (POOL_WINDOWS[2], POOL_WINDOWS[3]))):
        cs = slice(col * LANES, (col + 1) * LANES)
        x0 = pool_buf[POOL_HIST:POOL_HIST + T, cs]
        acc = x0
        for s in range(1, w_hi):
            xs = pool_buf[POOL_HIST - s:POOL_HIST - s + T, cs]
            acc = acc + (xs if s < w_lo else jnp.where(upper_half, xs, 0.0))
        window = jnp.where(upper_half, w_hi, w_lo)
        count = jnp.minimum(t_abs + 1, window).astype(_F32)
        pooled.append(acc / count - x0)
    pooled = jnp.concatenate(pooled, axis=1).astype(_BF16)
    pool_feat = (_dot(pooled, w_pool_ref[...]) * pool_scale_ref[...]).astype(_BF16)

    gate_blocks = N_BRANCH * D // COL_BLOCK
    for j in range(CHUNKS_PER_TILE):
        if j + 1 < CHUNKS_PER_TILE:
            scores(j + 1)
        weighted_values(j)
        first = j * gate_blocks // CHUNKS_PER_TILE
        gates(first, (j + 1) * gate_blocks // CHUNKS_PER_TILE - first)
        if j + 1 < CHUNKS_PER_TILE:
            softmax(j + 1)

    span = CONV_HIST + T - SUBLANES
    for r in range(1, SUBLANES):
        conv_shift[r - 1, 0:span, :] = conv_buf[r:r + span, :]
    acc = None
    for tap in range(CONV_WIDTH):
        start = CONV_HIST - (CONV_WIDTH - 1) + tap
        base, r = (start // SUBLANES) * SUBLANES, start % SUBLANES
        rows = conv_buf[base:base + T, :] if r == 0 else conv_shift[r - 1, base:base + T, :]
        term = conv_w_ref[tap:tap + 1, :] * rows
        acc = term if acc is None else acc + term
    hc = _normalize(acc + conv_b_ref[...]) * conv_g_ref[...] + conv_beta_ref[...]
    conv_feat = (hc * jax.nn.sigmoid(hc)).astype(_BF16)

    for cch in range(KV_HIST // CHUNK):
        dst = slice(cch * CHUNK, (cch + 1) * CHUNK)
        src = slice(T + cch * CHUNK, T + (cch + 1) * CHUNK)
        k_buf[dst, :] = k_buf[src, :]
        v_buf[dst, :] = v_buf[src, :]
    pool_buf[0:POOL_HIST, :] = pool_buf[T:T + POOL_HIST, :]
    conv_buf[0:CONV_HIST, :] = conv_buf[T:T + CONV_HIST, :]

    feats = (pool_feat, attn_buf[...], conv_feat)
    weights = (w_bp_ref, w_ba_ref, w_bc_ref)
    for nb in range(D // COL_BLOCK):
        cs = slice(nb * COL_BLOCK, (nb + 1) * COL_BLOCK)
        merged = None
        for br in range(N_BRANCH):
            gs = slice(br * D + nb * COL_BLOCK, br * D + (nb + 1) * COL_BLOCK)
            term = gate_buf[:, gs] * _dot(feats[br], weights[br][:, cs])
            merged = term if merged is None else merged + term
        merged_buf[:, cs] = merged.astype(_BF16)

    mix = _dot(merged_buf[...], w_o_ref[...])
    r = ALPHA * x + g_m * mix
    o_ref[...] = _normalize(r) * ln_g_ref[...] + ln_b_ref[...]


def _layer_spec(stacked, layer):
    index = (layer,) + (0,) * (stacked.ndim - 1)
    return pl.BlockSpec((None,) + stacked.shape[1:], lambda *_: index, pipeline_mode=pl.Buffered(1))


def _mod_spec(mod, layer):
    return pl.BlockSpec((None, None, 1, mod.shape[-1]), lambda b, i: (layer, b, 0, 0))


def _mixer_call(layer, x, mod, w_in, b_gate, w_pool_bd, pool_scale, bias_pairs, conv_w, conv_b,
                conv_g, conv_beta, w_bp, w_ba, w_bc, w_o, ln_g, ln_b):
    B, S, D = x.shape
    T = MIX_TILE
    consts = (w_in, b_gate, w_pool_bd, pool_scale, bias_pairs, conv_w, conv_b, conv_g, conv_beta,
              w_bp, w_ba, w_bc, w_o, ln_g, ln_b)
    return pl.pallas_call(
        _mixer_kernel,
        grid=(B, S // T),
        in_specs=[pl.BlockSpec((None, T, D), lambda b, i: (b, i, 0)), _mod_spec(mod, layer)]
                 + [_layer_spec(a, layer) for a in consts],
        out_specs=pl.BlockSpec((None, T, D), lambda b, i: (b, i, 0)),
        out_shape=jax.ShapeDtypeStruct((B, S, D), _F32),
        scratch_shapes=[
            pltpu.VMEM((T, D), _BF16),
            pltpu.VMEM((T, D_ATTN), _BF16),
            pltpu.VMEM((T, D_ATTN), _BF16),
            pltpu.VMEM((KV_HIST + T, D_ATTN), _BF16),
            pltpu.VMEM((KV_HIST + T, D_ATTN), _BF16),
            pltpu.VMEM((T, D_ATTN), _BF16),
            pltpu.VMEM((POOL_HIST + T, D_POOL), _F32),
            pltpu.VMEM((CONV_HIST + T, D_CONV), _F32),
            pltpu.VMEM((SUBLANES - 1, CONV_HIST + T, D_CONV), _F32),
            pltpu.VMEM((SCORE_SLOTS, 2 * CHUNK, BAND), _F32),
            pltpu.VMEM((SCORE_SLOTS, 2 * CHUNK, BAND), _BF16),
            pltpu.VMEM((SCORE_SLOTS, 2 * CHUNK, 1), _F32),
            pltpu.VMEM((T, N_BRANCH * D), _F32),
            pltpu.VMEM((T, D), _BF16),
        ],
        compiler_params=pltpu.CompilerParams(
            dimension_semantics=("arbitrary", "arbitrary"), vmem_limit_bytes=MIX_VMEM_LIMIT),
        name="token_mixer",
    )(x, mod, *consts)


def _ffn_kernel(x_ref, mod_ref, w1_ref, b1_ref, w2_ref, b2_ref, ln_g_ref, ln_b_ref, o_ref,
                u_even, u_odd, x_even, x_odd, r_even, r_odd, h_buf, *, n_tiles, tiles_per_seq):
    D = D_MODEL
    s = pl.program_id(0)
    n_blocks = D_FF // FF_BLOCK
    rows_per_slice = FFN_TILE // n_blocks

    def batch_of(tile):
        return lax.div(jnp.clip(tile, 0, n_tiles - 1), tiles_per_seq)

    def pre(rs, u_new, x_new):
        mod_new = mod_ref[batch_of(s)]
        x = x_ref[rs, :]
        x_new[rs, :] = x
        u = _normalize(x) * (1.0 + mod_new[:, 4 * D:5 * D]) + mod_new[:, 3 * D:4 * D]
        u_new[rs, :] = u.astype(_BF16)
        return u

    def post(rs, r_old):
        y = _normalize(r_old[rs, :]) * ln_g_ref[...] + ln_b_ref[...]
        o_ref[rs, :] = y
        return y

    def step(u_new, u_mid, x_new, x_mid, r_old, r_mid):
        mod_mid = mod_ref[batch_of(s - 1)]
        for c in range(n_blocks):
            rs = slice(c * rows_per_slice, (c + 1) * rows_per_slice)
            token = _zero_token(post(rs, r_old)) + _zero_token(pre(rs, u_new, x_new))
            cs = slice(c * FF_BLOCK, (c + 1) * FF_BLOCK)
            bias = b1_ref[:, cs] + jnp.concatenate([token] * (FF_BLOCK // LANES), axis=1)
            h = jnp.maximum(_dot(u_mid[...], w1_ref[:, cs]) + bias, 0.0)
            h_buf[:, cs] = (h * h).astype(_BF16)
        g_f = mod_mid[:, 5 * D:6 * D]
        for n in range(D // COL_BLOCK):
            ns = slice(n * COL_BLOCK, (n + 1) * COL_BLOCK)
            ff = _dot(h_buf[...], w2_ref[:, ns]) + b2_ref[:, ns]
            r_mid[:, ns] = ALPHA * x_mid[:, ns] + g_f[:, ns] * ff

    all_rows = slice(0, FFN_TILE)
    in_steady_state = jnp.logical_and(s >= 1, s <= n_tiles)
    parity = lax.rem(s, 2)
    r_last = r_odd if (n_tiles - 1) % 2 else r_even

    @pl.when(s == 0)
    def _():
        r_odd[...] = jnp.zeros(r_odd.shape, _F32)
        pre(all_rows, u_even, x_even)

    @pl.when(jnp.logical_and(in_steady_state, parity == 0))
    def _():
        step(u_even, u_odd, x_even, x_odd, r_even, r_odd)

    @pl.when(jnp.logical_and(in_steady_state, parity == 1))
    def _():
        step(u_odd, u_even, x_odd, x_even, r_odd, r_even)

    @pl.when(s == n_tiles + 1)
    def _():
        post(all_rows, r_last)


def _ffn_call(layer, x, mod, w1, b1, w2, b2, ln_g, ln_b):
    B, S, D = x.shape
    T = FFN_TILE
    n_tiles = B * S // T
    consts = (w1, b1, w2, b2, ln_g, ln_b)
    kernel_fn = functools.partial(_ffn_kernel, n_tiles=n_tiles, tiles_per_seq=S // T)
    out = pl.pallas_call(
        kernel_fn,
        grid=(n_tiles + 2,),
        in_specs=[pl.BlockSpec((T, D), lambda s: (jnp.minimum(s, n_tiles - 1), 0)),
                  pl.BlockSpec((None,) + mod.shape[1:], lambda s: (layer, 0, 0, 0))]
                 + [_layer_spec(a, layer) for a in consts],
        out_specs=pl.BlockSpec((T, D), lambda s: (jnp.clip(s - 2, 0, n_tiles - 1), 0)),
        out_shape=jax.ShapeDtypeStruct((B * S, D), _F32),
        scratch_shapes=[pltpu.VMEM((T, D), _BF16), pltpu.VMEM((T, D), _BF16),
                        pltpu.VMEM((T, D), _F32), pltpu.VMEM((T, D), _F32),
                        pltpu.VMEM((T, D), _F32), pltpu.VMEM((T, D), _F32),
                        pltpu.VMEM((T, D_FF), _BF16)],
        compiler_params=pltpu.CompilerParams(
            dimension_semantics=("arbitrary",), vmem_limit_bytes=FFN_VMEM_LIMIT),
        name="relu2_mlp",
    )(x.reshape(B * S, D), mod, *consts)
    return out.reshape(B, S, D)


def _bias_pairs(rel_bias):
    period = BAND + CHUNK - 1
    m = np.arange(period)
    ext_idx = np.clip(BAND - 1 - m, -REL_CLIP, REL_CLIP) + REL_CLIP
    ext = rel_bias[:, :, ext_idx] * LOG2_E
    lead = ext.shape[:-1]
    hankel = jnp.tile(ext, (1, 1, CHUNK + 1))[..., :CHUNK * (period + 1)]
    hankel = hankel.reshape(*lead, CHUNK, period + 1)[..., :BAND]
    bias = hankel[..., ::-1, :]
    return bias.reshape(lead[0], HEAD_PAIRS, 2 * CHUNK, BAND)


def _block_diag(w):
    depth, groups, gin, gout = w.shape
    eye = jnp.eye(groups, dtype=w.dtype)[None, :, None, :, None]
    return (w[:, :, :, None, :] * eye).reshape(depth, groups * gin, groups * gout)


def kernel(x, c, w_ada, b_ada, w_in, b_gate, w_pool, pool_scale, rel_bias, conv_w, conv_b, conv_ln_g, conv_ln_b, w_br_pool, w_br_attn, w_br_conv, w_o, ln_mix_g, ln_mix_b, w_ff1, b_ff1, w_ff2, b_ff2, ln_ff_g, ln_ff_b):
    depth = w_in.shape[0]
    bf16 = lambda a: a.astype(_BF16)
    rows = lambda a: a.reshape(depth, 1, -1)
    mod = _ada_call(c, w_ada, b_ada)[:, :, None, :]
    mixer_params = (bf16(w_in), rows(b_gate), bf16(_block_diag(w_pool)), rows(pool_scale),
                    _bias_pairs(rel_bias), conv_w, rows(conv_b), rows(conv_ln_g), rows(conv_ln_b),
                    bf16(w_br_pool), bf16(w_br_attn), bf16(w_br_conv), bf16(w_o),
                    rows(ln_mix_g), rows(ln_mix_b))
    ffn_params = (bf16(w_ff1), rows(b_ff1), bf16(w_ff2), rows(b_ff2), rows(ln_ff_g), rows(ln_ff_b))
    for layer in range(depth):
        x = _mixer_call(layer, x, mod, *mixer_params)
        x = _ffn_call(layer, x, mod, *ffn_params)
    return x
```

```python
import functools

import jax
import jax.numpy as jnp
import numpy as np
from jax import lax
from jax.experimental import pallas as pl
from jax.experimental.pallas import tpu as pltpu

D_MODEL = 1024
DEPTH = 2
CHUNK = 64
POOL_WINDOWS = (2, 4, 8, 16)
POOL_GROUP = 64
D_POOL = POOL_GROUP * len(POOL_WINDOWS)
N_HEADS = 8
HEAD_DIM = 64
D_ATTN = N_HEADS * HEAD_DIM
N_PREV_CHUNKS = 8
REL_CLIP = 128
D_CONV = 256
CONV_WIDTH = 31
D_FF = 4 * D_MODEL
N_BRANCH = 3
ALPHA = (2.0 * DEPTH) ** 0.25
LN_EPS = 1e-5
NEG_INF = -1e30
LOG2_E = 1.4426950408889634

OFF_POOL = 0
OFF_Q = OFF_POOL + D_POOL
OFF_K = OFF_Q + D_ATTN
OFF_V = OFF_K + D_ATTN
OFF_CONV = OFF_V + D_ATTN
OFF_GATE = OFF_CONV + 2 * D_CONV
D_IN = OFF_GATE + N_BRANCH * D_MODEL

LANES = 128
SUBLANES = 8
MIX_TILE = 512
CHUNKS_PER_TILE = MIX_TILE // CHUNK
BAND_CHUNKS = N_PREV_CHUNKS + 2
BAND = BAND_CHUNKS * CHUNK
KV_HIST = (BAND_CHUNKS - 1) * CHUNK
CONV_HIST = 32
POOL_HIST = 16
HEAD_PAIRS = N_HEADS // 2
SCORE_SLOTS = 2 * HEAD_PAIRS
COL_BLOCK = 256
FFN_TILE = 512
FF_BLOCK = 512
ADA_BLOCK = 1536
MIX_VMEM_LIMIT = 52 * 1024 * 1024
FFN_VMEM_LIMIT = 48 * 1024 * 1024
ADA_VMEM_LIMIT = 32 * 1024 * 1024

_F32 = jnp.float32
_BF16 = jnp.bfloat16


def _dot(a, b):
    return jnp.dot(a, b, preferred_element_type=_F32)


def _normalize(x):
    mu = jnp.mean(x, axis=-1, keepdims=True)
    xc = x - mu
    var = jnp.mean(xc * xc, axis=-1, keepdims=True)
    return xc * lax.rsqrt(var + LN_EPS)


def _zero_token(v):
    rows, cols = v.shape
    folded = v.reshape(rows // SUBLANES, SUBLANES, cols).sum(axis=0)
    folded = sum(folded[:, k * LANES:(k + 1) * LANES] for k in range(cols // LANES))
    bits = lax.bitcast_convert_type(folded, jnp.uint32)
    half = jnp.uint32(16)
    zero = lax.shift_right_logical(lax.shift_right_logical(bits, half), half)
    return lax.bitcast_convert_type(zero, _F32)[0:1, :]


def _ada_kernel(c_ref, w_ref, b_ref, o_ref):
    c = c_ref[...]
    ca = (c * jax.nn.sigmoid(c)).astype(_BF16)
    o_ref[...] = _dot(ca, w_ref[...].astype(_BF16)) + b_ref[...]


def _ada_call(c, w_ada, b_ada):
    depth, d, n = w_ada.shape
    batch = c.shape[0]
    return pl.pallas_call(
        _ada_kernel,
        grid=(depth, n // ADA_BLOCK),
        in_specs=[
            pl.BlockSpec((batch, d), lambda l, j: (0, 0)),
            pl.BlockSpec((None, d, ADA_BLOCK), lambda l, j: (l, 0, j)),
            pl.BlockSpec((None, 1, ADA_BLOCK), lambda l, j: (l, 0, j)),
        ],
        out_specs=pl.BlockSpec((None, batch, ADA_BLOCK), lambda l, j: (l, 0, j)),
        out_shape=jax.ShapeDtypeStruct((depth, batch, n), _F32),
        compiler_params=pltpu.CompilerParams(
            dimension_semantics=("arbitrary", "arbitrary"), vmem_limit_bytes=ADA_VMEM_LIMIT),
        name="ada_mod",
    )(c, w_ada, b_ada.reshape(depth, 1, n))


def _mixer_kernel(x_ref, mod_ref, w_in_ref, b_gate_ref, w_pool_ref, pool_scale_ref, bias_ref,
                  conv_w_ref, conv_b_ref, conv_g_ref, conv_beta_ref, w_bp_ref, w_ba_ref, w_bc_ref,
                  w_o_ref, o_ref,
                  u_buf, qe_buf, qo_buf, k_buf, v_buf, attn_buf, pool_buf, conv_buf, conv_shift,
                  score_buf, prob_buf, rdenom_buf, gate_buf, merged_buf):
    T, D = MIX_TILE, D_MODEL
    i = pl.program_id(1)

    @pl.when(i == 0)
    def _():
        k_buf[0:KV_HIST, :] = jnp.zeros((KV_HIST, D_ATTN), _BF16)
        v_buf[0:KV_HIST, :] = jnp.zeros((KV_HIST, D_ATTN), _BF16)
        pool_buf[0:POOL_HIST, :] = jnp.zeros((POOL_HIST, D_POOL), _F32)
        conv_buf[0:CONV_HIST, :] = jnp.zeros((CONV_HIST, D_CONV), _F32)

    sh_m = mod_ref[:, 0:D]
    sc_m = mod_ref[:, D:2 * D]
    g_m = mod_ref[:, 2 * D:3 * D]
    u_buf[...] = (_normalize(x_ref[...]) * (1.0 + sc_m) + sh_m).astype(_BF16)

    def in_proj(off, width):
        return _dot(u_buf[...], w_in_ref[:, off:off + width])

    lane = lax.broadcasted_iota(jnp.int32, (CHUNK, LANES), 1)
    key_idx = lax.broadcasted_iota(jnp.int32, (1, BAND), 1)
    dyn0 = jnp.minimum(i, 0)

    def scores(j):
        rows = slice(j * CHUNK, (j + 1) * CHUNK)
        band = slice(j * CHUNK, j * CHUNK + BAND)
        for p in range(HEAD_PAIRS):
            cs = slice(p * LANES, (p + 1) * LANES)
            qs = jnp.concatenate([qe_buf[rows, cs], qo_buf[rows, cs]], axis=0)
            score_buf[(j % 2) * HEAD_PAIRS + p + dyn0] = lax.dot_general(
                qs, k_buf[band, cs], (((1,), (1,)), ((), ())), preferred_element_type=_F32)

    def softmax(j):
        chunk = i * CHUNKS_PER_TILE + j
        first_valid = jnp.maximum(BAND_CHUNKS - 1 - chunk, 1) * CHUNK
        mask_add = jnp.where(key_idx >= first_valid, 0.0, NEG_INF)
        for p in range(HEAD_PAIRS):
            slot = (j % 2) * HEAD_PAIRS + p + dyn0
            sc = score_buf[slot] + bias_ref[p] + mask_add
            e = jnp.exp2(sc - jnp.max(sc, axis=-1, keepdims=True))
            rdenom_buf[slot] = 1.0 / jnp.sum(e, axis=-1, keepdims=True)
            prob_buf[slot] = e.astype(_BF16)

    def weighted_values(j):
        rows = slice(j * CHUNK, (j + 1) * CHUNK)
        band = slice(j * CHUNK, j * CHUNK + BAND)
        for p in range(HEAD_PAIRS):
            cs = slice(p * LANES, (p + 1) * LANES)
            slot = (j % 2) * HEAD_PAIRS + p + dyn0
            pv = _dot(prob_buf[slot], v_buf[band, cs]) * rdenom_buf[slot]
            attn_buf[rows, cs] = jnp.where(lane < HEAD_DIM, pv[0:CHUNK], pv[CHUNK:2 * CHUNK]).astype(_BF16)

    def gates(first, count):
        for g in range(first, first + count):
            cs = slice(g * COL_BLOCK, (g + 1) * COL_BLOCK)
            gate_buf[:, cs] = jax.nn.sigmoid(in_proj(OFF_GATE + g * COL_BLOCK, COL_BLOCK) + b_gate_ref[:, cs])

    q = in_proj(OFF_Q, D_ATTN) * (HEAD_DIM ** -0.5 * LOG2_E)
    even_head = (lax.broadcasted_iota(jnp.int32, (T, D_ATTN), 1) & (LANES - 1)) < HEAD_DIM
    qe_buf[...] = jnp.where(even_head, q, 0.0).astype(_BF16)
    qo_buf[...] = jnp.where(even_head, 0.0, q).astype(_BF16)
    k_buf[KV_HIST:KV_HIST + T, :] = in_proj(OFF_K, D_ATTN).astype(_BF16)
    v_buf[KV_HIST:KV_HIST + T, :] = in_proj(OFF_V, D_ATTN).astype(_BF16)
    scores(0)

    pool_buf[POOL_HIST:POOL_HIST + T, :] = in_proj(OFF_POOL, D_POOL)
    zc = in_proj(OFF_CONV, 2 * D_CONV)
    conv_buf[CONV_HIST:CONV_HIST + T, :] = zc[:, 0:D_CONV] * jax.nn.sigmoid(zc[:, D_CONV:2 * D_CONV])
    softmax(0)

    t_abs = i * T + lax.broadcasted_iota(jnp.int32, (T, LANES), 0)
    upper_half = lax.broadcasted_iota(jnp.int32, (T, LANES), 1) >= POOL_GROUP
    pooled = []
    for col, (w_lo, w_hi) in enumerate(((POOL_WINDOWS[0], POOL_WINDOWS[1]), (POOL_WINDOWS[2], POOL_WINDOWS[3]))):
        cs = slice(col * LANES, (col + 1) * LANES)
        x0 = pool_buf[POOL_HIST:POOL_HIST + T, cs]
        acc = x0
        for sh in range(1, w_hi):
            xs = pool_buf[POOL_HIST - sh:POOL_HIST - sh + T, cs]
            acc = acc + (xs if sh < w_lo else jnp.where(upper_half, xs, 0.0))
        window = jnp.where(upper_half, w_hi, w_lo)
        count = jnp.minimum(t_abs + 1, window).astype(_F32)
        pooled.append(acc / count - x0)
    pooled = jnp.concatenate(pooled, axis=1).astype(_BF16)
    pool_feat = (_dot(pooled, w_pool_ref[...]) * pool_scale_ref[...]).astype(_BF16)

    gate_blocks = N_BRANCH * D // COL_BLOCK
    for j in range(CHUNKS_PER_TILE):
        if j + 1 < CHUNKS_PER_TILE:
            scores(j + 1)
        weighted_values(j)
        first = j * gate_blocks // CHUNKS_PER_TILE
        gates(first, (j + 1) * gate_blocks // CHUNKS_PER_TILE - first)
        if j + 1 < CHUNKS_PER_TILE:
            softmax(j + 1)

    span = CONV_HIST + T - SUBLANES
    for r in range(1, SUBLANES):
        conv_shift[r - 1, 0:span, :] = conv_buf[r:r + span, :]
    acc = None
    for tap in range(CONV_WIDTH):
        start = CONV_HIST - (CONV_WIDTH - 1) + tap
        base, r = (start // SUBLANES) * SUBLANES, start % SUBLANES
        rows = conv_buf[base:base + T, :] if r == 0 else conv_shift[r - 1, base:base + T, :]
        term = conv_w_ref[tap:tap + 1, :] * rows
        acc = term if acc is None else acc + term
    hc = _normalize(acc + conv_b_ref[...]) * conv_g_ref[...] + conv_beta_ref[...]
    conv_feat = (hc * jax.nn.sigmoid(hc)).astype(_BF16)

    for cch in range(KV_HIST // CHUNK):
        dst = slice(cch * CHUNK, (cch + 1) * CHUNK)
        src = slice(T + cch * CHUNK, T + (cch + 1) * CHUNK)
        k_buf[dst, :] = k_buf[src, :]
        v_buf[dst, :] = v_buf[src, :]
    pool_buf[0:POOL_HIST, :] = pool_buf[T:T + POOL_HIST, :]
    conv_buf[0:CONV_HIST, :] = conv_buf[T:T + CONV_HIST, :]

    feats = (pool_feat, attn_buf[...], conv_feat)
    weights = (w_bp_ref, w_ba_ref, w_bc_ref)
    col_blocks = D // COL_BLOCK
    for nb in range(col_blocks):
        cs = slice(nb * COL_BLOCK, (nb + 1) * COL_BLOCK)
        merged = None
        for br in range(N_BRANCH):
            gs = slice(br * D + nb * COL_BLOCK, br * D + (nb + 1) * COL_BLOCK)
            term = gate_buf[:, gs] * _dot(feats[br], weights[br][:, cs])
            merged = term if merged is None else merged + term
        merged_buf[:, cs] = merged.astype(_BF16)

    for nb in range(col_blocks):
        cs = slice(nb * COL_BLOCK, (nb + 1) * COL_BLOCK)
        o_ref[:, cs] = ALPHA * x_ref[:, cs] + g_m[:, cs] * _dot(merged_buf[...], w_o_ref[:, cs])


def _layer_spec(stacked, layer):
    index = (layer,) + (0,) * (stacked.ndim - 1)
    return pl.BlockSpec((None,) + stacked.shape[1:], lambda *_: index, pipeline_mode=pl.Buffered(1))


def _mixer_call(layer, x, mod, w_in, b_gate, w_pool_bd, pool_scale, bias_pairs, conv_w, conv_b,
                conv_g, conv_beta, w_bp, w_ba, w_bc, w_o):
    B, S, D = x.shape
    T = MIX_TILE
    consts = (w_in, b_gate, w_pool_bd, pool_scale, bias_pairs, conv_w, conv_b, conv_g, conv_beta,
              w_bp, w_ba, w_bc, w_o)
    return pl.pallas_call(
        _mixer_kernel,
        grid=(B, S // T),
        in_specs=[pl.BlockSpec((None, T, D), lambda b, i: (b, i, 0)),
                  pl.BlockSpec((None, None, 1, mod.shape[-1]), lambda b, i: (layer, b, 0, 0))]
                 + [_layer_spec(a, layer) for a in consts],
        out_specs=pl.BlockSpec((None, T, D), lambda b, i: (b, i, 0)),
        out_shape=jax.ShapeDtypeStruct((B, S, D), _F32),
        scratch_shapes=[
            pltpu.VMEM((T, D), _BF16),
            pltpu.VMEM((T, D_ATTN), _BF16),
            pltpu.VMEM((T, D_ATTN), _BF16),
            pltpu.VMEM((KV_HIST + T, D_ATTN), _BF16),
            pltpu.VMEM((KV_HIST + T, D_ATTN), _BF16),
            pltpu.VMEM((T, D_ATTN), _BF16),
            pltpu.VMEM((POOL_HIST + T, D_POOL), _F32),
            pltpu.VMEM((CONV_HIST + T, D_CONV), _F32),
            pltpu.VMEM((SUBLANES - 1, CONV_HIST + T, D_CONV), _F32),
            pltpu.VMEM((SCORE_SLOTS, 2 * CHUNK, BAND), _F32),
            pltpu.VMEM((SCORE_SLOTS, 2 * CHUNK, BAND), _BF16),
            pltpu.VMEM((SCORE_SLOTS, 2 * CHUNK, 1), _F32),
            pltpu.VMEM((T, N_BRANCH * D), _F32),
            pltpu.VMEM((T, D), _BF16),
        ],
        compiler_params=pltpu.CompilerParams(
            dimension_semantics=("arbitrary", "arbitrary"), vmem_limit_bytes=MIX_VMEM_LIMIT),
        name="token_mixer",
    )(x, mod, *consts)


def _ffn_kernel(r_ref, mod_ref, mix_g_ref, mix_b_ref, w1_ref, b1_ref, w2_ref, b2_ref, ln_g_ref,
                ln_b_ref, o_ref,
                u_even, u_odd, x_even, x_odd, r_even, r_odd, h_buf, *, n_tiles, tiles_per_seq):
    D = D_MODEL
    s = pl.program_id(0)
    n_blocks = D_FF // FF_BLOCK
    rows_per_slice = FFN_TILE // n_blocks

    def batch_of(tile):
        return lax.div(jnp.clip(tile, 0, n_tiles - 1), tiles_per_seq)

    def pre(rs, u_new, x_new):
        mod_new = mod_ref[batch_of(s)]
        x = _normalize(r_ref[rs, :]) * mix_g_ref[...] + mix_b_ref[...]
        x_new[rs, :] = x
        u = _normalize(x) * (1.0 + mod_new[:, 4 * D:5 * D]) + mod_new[:, 3 * D:4 * D]
        u_new[rs, :] = u.astype(_BF16)
        return u

    def post(rs, r_old):
        y = _normalize(r_old[rs, :]) * ln_g_ref[...] + ln_b_ref[...]
        o_ref[rs, :] = y
        return y

    def step(u_new, u_mid, x_new, x_mid, r_old, r_mid):
        mod_mid = mod_ref[batch_of(s - 1)]
        for c in range(n_blocks):
            rs = slice(c * rows_per_slice, (c + 1) * rows_per_slice)
            token = _zero_token(post(rs, r_old)) + _zero_token(pre(rs, u_new, x_new))
            cs = slice(c * FF_BLOCK, (c + 1) * FF_BLOCK)
            bias = b1_ref[:, cs] + jnp.concatenate([token] * (FF_BLOCK // LANES), axis=1)
            h = jnp.maximum(_dot(u_mid[...], w1_ref[:, cs]) + bias, 0.0)
            h_buf[:, cs] = (h * h).astype(_BF16)
        g_f = mod_mid[:, 5 * D:6 * D]
        for n in range(D // COL_BLOCK):
            ns = slice(n * COL_BLOCK, (n + 1) * COL_BLOCK)
            ff = _dot(h_buf[...], w2_ref[:, ns]) + b2_ref[:, ns]
            r_mid[:, ns] = ALPHA * x_mid[:, ns] + g_f[:, ns] * ff

    all_rows = slice(0, FFN_TILE)
    in_steady_state = jnp.logical_and(s >= 1, s <= n_tiles)
    parity = lax.rem(s, 2)
    r_last = r_odd if (n_tiles - 1) % 2 else r_even

    @pl.when(s == 0)
    def _():
        r_odd[...] = jnp.zeros(r_odd.shape, _F32)
        pre(all_rows, u_even, x_even)

    @pl.when(jnp.logical_and(in_steady_state, parity == 0))
    def _():
        step(u_even, u_odd, x_even, x_odd, r_even, r_odd)

    @pl.when(jnp.logical_and(in_steady_state, parity == 1))
    def _():
        step(u_odd, u_even, x_odd, x_even, r_odd, r_even)

    @pl.when(s == n_tiles + 1)
    def _():
        post(all_rows, r_last)


def _tile_spec(tile, n_tiles, lag):
    return pl.BlockSpec((tile, D_MODEL), lambda s: (jnp.clip(s - lag, 0, n_tiles - 1), 0))


def _ffn_call(layer, r, mod, mix_g, mix_b, w1, b1, w2, b2, ln_g, ln_b):
    B, S, D = r.shape
    T = FFN_TILE
    n_tiles = B * S // T
    consts = (mix_g, mix_b, w1, b1, w2, b2, ln_g, ln_b)
    kernel_fn = functools.partial(_ffn_kernel, n_tiles=n_tiles, tiles_per_seq=S // T)
    out = pl.pallas_call(
        kernel_fn,
        grid=(n_tiles + 2,),
        in_specs=[_tile_spec(T, n_tiles, 0),
                  pl.BlockSpec((None,) + mod.shape[1:], lambda s: (layer, 0, 0, 0))]
                 + [_layer_spec(a, layer) for a in consts],
        out_specs=_tile_spec(T, n_tiles, 2),
        out_shape=jax.ShapeDtypeStruct((B * S, D), _F32),
        scratch_shapes=[pltpu.VMEM((T, D), _BF16), pltpu.VMEM((T, D), _BF16),
                        pltpu.VMEM((T, D), _F32), pltpu.VMEM((T, D), _F32),
                        pltpu.VMEM((T, D), _F32), pltpu.VMEM((T, D), _F32),
                        pltpu.VMEM((T, D_FF), _BF16)],
        compiler_params=pltpu.CompilerParams(
            dimension_semantics=("arbitrary",), vmem_limit_bytes=FFN_VMEM_LIMIT),
        name="relu2_mlp",
    )(r.reshape(B * S, D), mod, *consts)
    return out.reshape(B, S, D)


def _bias_pairs(rel_bias):
    period = BAND + CHUNK - 1
    m = np.arange(period)
    ext_idx = np.clip(BAND - 1 - m, -REL_CLIP, REL_CLIP) + REL_CLIP
    ext = rel_bias[:, :, ext_idx] * LOG2_E
    lead = ext.shape[:-1]
    hankel = jnp.tile(ext, (1, 1, CHUNK + 1))[..., :CHUNK * (period + 1)]
    hankel = hankel.reshape(*lead, CHUNK, period + 1)[..., :BAND]
    bias = hankel[..., ::-1, :]
    return bias.reshape(lead[0], HEAD_PAIRS, 2 * CHUNK, BAND)


def _block_diag(w):
    depth, groups, gin, gout = w.shape
    eye = jnp.eye(groups, dtype=w.dtype)[None, :, None, :, None]
    return (w[:, :, :, None, :] * eye).reshape(depth, groups * gin, groups * gout)


def kernel(x, c, w_ada, b_ada, w_in, b_gate, w_pool, pool_scale, rel_bias, conv_w, conv_b, conv_ln_g, conv_ln_b, w_br_pool, w_br_attn, w_br_conv, w_o, ln_mix_g, ln_mix_b, w_ff1, b_ff1, w_ff2, b_ff2, ln_ff_g, ln_ff_b):
    depth = w_in.shape[0]
    bf16 = lambda a: a.astype(_BF16)
    rows = lambda a: a.reshape(depth, 1, -1)
    mod = _ada_call(c, w_ada, b_ada)[:, :, None, :]
    mixer_params = (bf16(w_in), rows(b_gate), bf16(_block_diag(w_pool)), rows(pool_scale),
                    _bias_pairs(rel_bias), conv_w, rows(conv_b), rows(conv_ln_g), rows(conv_ln_b),
                    bf16(w_br_pool), bf16(w_br_attn), bf16(w_br_conv), bf16(w_o))
    ffn_params = (rows(ln_mix_g), rows(ln_mix_b), bf16(w_ff1), rows(b_ff1), bf16(w_ff2),
                  rows(b_ff2), rows(ln_ff_g), rows(ln_ff_b))
    for layer in range(depth):
        r = _mixer_call(layer, x, mod, *mixer_params)
        x = _ffn_call(layer, r, mod, *ffn_params)
    return x
```

```python
import functools

import jax
import jax.numpy as jnp
import numpy as np
from jax import lax
from jax.experimental import pallas as pl
from jax.experimental.pallas import tpu as pltpu

D_MODEL = 1024
DEPTH = 2
CHUNK = 64
POOL_WINDOWS = (2, 4, 8, 16)
POOL_GROUP = 64
D_POOL = POOL_GROUP * len(POOL_WINDOWS)
N_HEADS = 8
HEAD_DIM = 64
D_ATTN = N_HEADS * HEAD_DIM
N_PREV_CHUNKS = 8
REL_CLIP = 128
D_CONV = 256
CONV_WIDTH = 31
D_FF = 4 * D_MODEL
N_BRANCH = 3
ALPHA = (2.0 * DEPTH) ** 0.25
LN_EPS = 1e-5
NEG_INF = -1e30
LOG2_E = 1.4426950408889634

OFF_POOL = 0
OFF_Q = OFF_POOL + D_POOL
OFF_K = OFF_Q + D_ATTN
OFF_V = OFF_K + D_ATTN
OFF_CONV = OFF_V + D_ATTN
OFF_GATE = OFF_CONV + 2 * D_CONV
D_IN = OFF_GATE + N_BRANCH * D_MODEL

LANES = 128
SUBLANES = 8
MIX_TILE = 512
CHUNKS_PER_TILE = MIX_TILE // CHUNK
BAND_CHUNKS = N_PREV_CHUNKS + 2
BAND = BAND_CHUNKS * CHUNK
KV_HIST = (BAND_CHUNKS - 1) * CHUNK
CONV_HIST = 32
POOL_HIST = 16
HEAD_PAIRS = N_HEADS // 2
SCORE_SLOTS = 2 * HEAD_PAIRS
COL_BLOCK = 256
FFN_TILE = 512
FF_BLOCK = 512
ADA_BLOCK = 1536
MIX_VMEM_LIMIT = 52 * 1024 * 1024
FFN_VMEM_LIMIT = 48 * 1024 * 1024
ADA_VMEM_LIMIT = 32 * 1024 * 1024

_F32 = jnp.float32
_BF16 = jnp.bfloat16


def _dot(a, b):
    return jnp.dot(a, b, preferred_element_type=_F32)


def _normalize(x):
    mu = jnp.mean(x, axis=-1, keepdims=True)
    xc = x - mu
    var = jnp.mean(xc * xc, axis=-1, keepdims=True)
    return xc * lax.rsqrt(var + LN_EPS)


def _zero_token(v):
    rows, cols = v.shape
    folded = v.reshape(rows // SUBLANES, SUBLANES, cols).sum(axis=0)
    folded = sum(folded[:, k * LANES:(k + 1) * LANES] for k in range(cols // LANES))
    bits = lax.bitcast_convert_type(folded, jnp.uint32)
    half = jnp.uint32(16)
    zero = lax.shift_right_logical(lax.shift_right_logical(bits, half), half)
    return lax.bitcast_convert_type(zero, _F32)[0:1, :]


def _ada_kernel(c_ref, w_ref, b_ref, o_ref):
    c = c_ref[...]
    ca = (c * jax.nn.sigmoid(c)).astype(_BF16)
    o_ref[...] = _dot(ca, w_ref[...].astype(_BF16)) + b_ref[...]


def _ada_call(c, w_ada, b_ada):
    depth, d, n = w_ada.shape
    batch = c.shape[0]
    return pl.pallas_call(
        _ada_kernel,
        grid=(depth, n // ADA_BLOCK),
        in_specs=[
            pl.BlockSpec((batch, d), lambda l, j: (0, 0)),
            pl.BlockSpec((None, d, ADA_BLOCK), lambda l, j: (l, 0, j)),
            pl.BlockSpec((None, 1, ADA_BLOCK), lambda l, j: (l, 0, j)),
        ],
        out_specs=pl.BlockSpec((None, batch, ADA_BLOCK), lambda l, j: (l, 0, j)),
        out_shape=jax.ShapeDtypeStruct((depth, batch, n), _F32),
        compiler_params=pltpu.CompilerParams(
            dimension_semantics=("arbitrary", "arbitrary"), vmem_limit_bytes=ADA_VMEM_LIMIT),
        name="ada_mod",
    )(c, w_ada, b_ada.reshape(depth, 1, n))


def _mixer_kernel(x_ref, mod_ref, w_in_ref, b_gate_ref, w_pool_ref, pool_scale_ref, bias_ref,
                  conv_w_ref, conv_b_ref, conv_g_ref, conv_beta_ref, w_bp_ref, w_ba_ref, w_bc_ref,
                  w_o_ref, ln_g_ref, ln_b_ref, o_ref,
                  u_buf, qe_buf, qo_buf, k_buf, v_buf, attn_buf, pool_buf, conv_buf, conv_shift,
                  score_buf, rowmax_buf, prob_buf, gate_buf, merged_buf):
    T, D = MIX_TILE, D_MODEL
    i = pl.program_id(1)

    @pl.when(i == 0)
    def _():
        k_buf[0:KV_HIST, :] = jnp.zeros((KV_HIST, D_ATTN), _BF16)
        for p in range(HEAD_PAIRS):
            v_buf[0:KV_HIST, 2 * p * LANES:(2 * p + 1) * LANES] = jnp.zeros((KV_HIST, LANES), _BF16)
            v_buf[:, (2 * p + 1) * LANES:2 * (p + 1) * LANES] = jnp.ones((KV_HIST + MIX_TILE, LANES), _BF16)
        pool_buf[0:POOL_HIST, :] = jnp.zeros((POOL_HIST, D_POOL), _F32)
        conv_buf[0:CONV_HIST, :] = jnp.zeros((CONV_HIST, D_CONV), _F32)

    sh_m = mod_ref[:, 0:D]
    sc_m = mod_ref[:, D:2 * D]
    g_m = mod_ref[:, 2 * D:3 * D]
    u_buf[...] = (_normalize(x_ref[...]) * (1.0 + sc_m) + sh_m).astype(_BF16)

    def in_proj(off, width):
        return _dot(u_buf[...], w_in_ref[:, off:off + width])

    lane = lax.broadcasted_iota(jnp.int32, (CHUNK, LANES), 1)
    key_idx = lax.broadcasted_iota(jnp.int32, (1, BAND), 1)
    dyn0 = jnp.minimum(i, 0)

    def scores(j):
        chunk = i * CHUNKS_PER_TILE + j
        first_valid = jnp.maximum(BAND_CHUNKS - 1 - chunk, 1) * CHUNK
        mask_add = jnp.where(key_idx >= first_valid, 0.0, NEG_INF)
        rows = slice(j * CHUNK, (j + 1) * CHUNK)
        band = slice(j * CHUNK, j * CHUNK + BAND)
        for p in range(HEAD_PAIRS):
            cs = slice(p * LANES, (p + 1) * LANES)
            slot = (j % 2) * HEAD_PAIRS + p + dyn0
            qs = jnp.concatenate([qe_buf[rows, cs], qo_buf[rows, cs]], axis=0)
            sc = lax.dot_general(qs, k_buf[band, cs], (((1,), (1,)), ((), ())),
                                 preferred_element_type=_F32) + bias_ref[p] + mask_add
            score_buf[slot] = sc
            rowmax_buf[slot] = jnp.max(sc, axis=-1, keepdims=True)

    def softmax(j):
        for p in range(HEAD_PAIRS):
            slot = (j % 2) * HEAD_PAIRS + p + dyn0
            prob_buf[slot] = jnp.exp2(score_buf[slot] - rowmax_buf[slot]).astype(_BF16)

    def weighted_values(j):
        rows = slice(j * CHUNK, (j + 1) * CHUNK)
        band = slice(j * CHUNK, j * CHUNK + BAND)
        for p in range(HEAD_PAIRS):
            slot = (j % 2) * HEAD_PAIRS + p + dyn0
            pv = _dot(prob_buf[slot], v_buf[band, 2 * p * LANES:2 * (p + 1) * LANES])
            pv = pv[:, 0:LANES] * (1.0 / pv[:, LANES:2 * LANES])
            attn_buf[rows, p * LANES:(p + 1) * LANES] = jnp.where(
                lane < HEAD_DIM, pv[0:CHUNK], pv[CHUNK:2 * CHUNK]).astype(_BF16)

    def gates(first, count):
        for g in range(first, first + count):
            cs = slice(g * COL_BLOCK, (g + 1) * COL_BLOCK)
            gate_buf[:, cs] = jax.nn.sigmoid(in_proj(OFF_GATE + g * COL_BLOCK, COL_BLOCK) + b_gate_ref[:, cs])

    q = in_proj(OFF_Q, D_ATTN) * (HEAD_DIM ** -0.5 * LOG2_E)
    even_head = (lax.broadcasted_iota(jnp.int32, (T, D_ATTN), 1) & (LANES - 1)) < HEAD_DIM
    qe_buf[...] = jnp.where(even_head, q, 0.0).astype(_BF16)
    qo_buf[...] = jnp.where(even_head, 0.0, q).astype(_BF16)
    k_buf[KV_HIST:KV_HIST + T, :] = in_proj(OFF_K, D_ATTN).astype(_BF16)
    v = in_proj(OFF_V, D_ATTN).astype(_BF16)
    for p in range(HEAD_PAIRS):
        v_buf[KV_HIST:KV_HIST + T, 2 * p * LANES:(2 * p + 1) * LANES] = v[:, p * LANES:(p + 1) * LANES]
    scores(0)

    pool_buf[POOL_HIST:POOL_HIST + T, :] = in_proj(OFF_POOL, D_POOL)
    zc = in_proj(OFF_CONV, 2 * D_CONV)
    conv_buf[CONV_HIST:CONV_HIST + T, :] = zc[:, 0:D_CONV] * jax.nn.sigmoid(zc[:, D_CONV:2 * D_CONV])
    softmax(0)

    t_abs = i * T + lax.broadcasted_iota(jnp.int32, (T, LANES), 0)
    upper_half = lax.broadcasted_iota(jnp.int32, (T, LANES), 1) >= POOL_GROUP
    pooled = []
    for col, (w_lo, w_hi) in enumerate(((POOL_WINDOWS[0], POOL_WINDOWS[1]), (POOL_WINDOWS[2], POOL_WINDOWS[3]))):
        cs = slice(col * LANES, (col + 1) * LANES)
        x0 = pool_buf[POOL_HIST:POOL_HIST + T, cs]
        acc = x0
        for sh in range(1, w_hi):
            xs = pool_buf[POOL_HIST - sh:POOL_HIST - sh + T, cs]
            acc = acc + (xs if sh < w_lo else jnp.where(upper_half, xs, 0.0))
        window = jnp.where(upper_half, w_hi, w_lo)
        count = jnp.minimum(t_abs + 1, window).astype(_F32)
        pooled.append(acc / count - x0)
    pooled = jnp.concatenate(pooled, axis=1).astype(_BF16)
    pool_feat = (_dot(pooled, w_pool_ref[...]) * pool_scale_ref[...]).astype(_BF16)

    gate_blocks = N_BRANCH * D // COL_BLOCK
    for j in range(CHUNKS_PER_TILE):
        if j + 1 < CHUNKS_PER_TILE:
            scores(j + 1)
        weighted_values(j)
        first = j * gate_blocks // CHUNKS_PER_TILE
        gates(first, (j + 1) * gate_blocks // CHUNKS_PER_TILE - first)
        if j + 1 < CHUNKS_PER_TILE:
            softmax(j + 1)

    span = CONV_HIST + T - SUBLANES
    for r in range(1, SUBLANES):
        conv_shift[r - 1, 0:span, :] = conv_buf[r:r + span, :]
    acc = None
    for tap in range(CONV_WIDTH):
        start = CONV_HIST - (CONV_WIDTH - 1) + tap
        base, r = (start // SUBLANES) * SUBLANES, start % SUBLANES
        rows = conv_buf[base:base + T, :] if r == 0 else conv_shift[r - 1, base:base + T, :]
        term = conv_w_ref[tap:tap + 1, :] * rows
        acc = term if acc is None else acc + term
    hc = _normalize(acc + conv_b_ref[...]) * conv_g_ref[...] + conv_beta_ref[...]
    conv_feat = (hc * jax.nn.sigmoid(hc)).astype(_BF16)

    for cch in range(KV_HIST // CHUNK):
        dst = slice(cch * CHUNK, (cch + 1) * CHUNK)
        src = slice(T + cch * CHUNK, T + (cch + 1) * CHUNK)
        k_buf[dst, :] = k_buf[src, :]
        for p in range(HEAD_PAIRS):
            vs = slice(2 * p * LANES, (2 * p + 1) * LANES)
            v_buf[dst, vs] = v_buf[src, vs]
    pool_buf[0:POOL_HIST, :] = pool_buf[T:T + POOL_HIST, :]
    conv_buf[0:CONV_HIST, :] = conv_buf[T:T + CONV_HIST, :]

    feats = (pool_feat, attn_buf[...], conv_feat)
    weights = (w_bp_ref, w_ba_ref, w_bc_ref)
    col_blocks = D // COL_BLOCK
    for nb in range(col_blocks):
        cs = slice(nb * COL_BLOCK, (nb + 1) * COL_BLOCK)
        merged = None
        for br in range(N_BRANCH):
            gs = slice(br * D + nb * COL_BLOCK, br * D + (nb + 1) * COL_BLOCK)
            term = gate_buf[:, gs] * _dot(feats[br], weights[br][:, cs])
            merged = term if merged is None else merged + term
        merged_buf[:, cs] = merged.astype(_BF16)

    mix = _dot(merged_buf[...], w_o_ref[...])
    r = ALPHA * x_ref[...] + g_m * mix
    o_ref[...] = _normalize(r) * ln_g_ref[...] + ln_b_ref[...]


def _layer_spec(stacked, layer):
    index = (layer,) + (0,) * (stacked.ndim - 1)
    return pl.BlockSpec((None,) + stacked.shape[1:], lambda *_: index, pipeline_mode=pl.Buffered(1))


def _mixer_call(layer, x, mod, w_in, b_gate, w_pool_bd, pool_scale, bias_pairs, conv_w, conv_b,
                conv_g, conv_beta, w_bp, w_ba, w_bc, w_o, ln_g, ln_b):
    B, S, D = x.shape
    T = MIX_TILE
    consts = (w_in, b_gate, w_pool_bd, pool_scale, bias_pairs, conv_w, conv_b, conv_g, conv_beta,
              w_bp, w_ba, w_bc, w_o, ln_g, ln_b)
    return pl.pallas_call(
        _mixer_kernel,
        grid=(B, S // T),
        in_specs=[pl.BlockSpec((None, T, D), lambda b, i: (b, i, 0)),
                  pl.BlockSpec((None, None, 1, mod.shape[-1]), lambda b, i: (layer, b, 0, 0))]
                 + [_layer_spec(a, layer) for a in consts],
        out_specs=pl.BlockSpec((None, T, D), lambda b, i: (b, i, 0)),
        out_shape=jax.ShapeDtypeStruct((B, S, D), _F32),
        scratch_shapes=[
            pltpu.VMEM((T, D), _BF16),
            pltpu.VMEM((T, D_ATTN), _BF16),
            pltpu.VMEM((T, D_ATTN), _BF16),
            pltpu.VMEM((KV_HIST + T, D_ATTN), _BF16),
            pltpu.VMEM((KV_HIST + T, 2 * D_ATTN), _BF16),
            pltpu.VMEM((T, D_ATTN), _BF16),
            pltpu.VMEM((POOL_HIST + T, D_POOL), _F32),
            pltpu.VMEM((CONV_HIST + T, D_CONV), _F32),
            pltpu.VMEM((SUBLANES - 1, CONV_HIST + T, D_CONV), _F32),
            pltpu.VMEM((SCORE_SLOTS, 2 * CHUNK, BAND), _F32),
            pltpu.VMEM((SCORE_SLOTS, 2 * CHUNK, 1), _F32),
            pltpu.VMEM((SCORE_SLOTS, 2 * CHUNK, BAND), _BF16),
            pltpu.VMEM((T, N_BRANCH * D), _F32),
            pltpu.VMEM((T, D), _BF16),
        ],
        compiler_params=pltpu.CompilerParams(
            dimension_semantics=("arbitrary", "arbitrary"), vmem_limit_bytes=MIX_VMEM_LIMIT),
        name="token_mixer",
    )(x, mod, *consts)


def _ffn_kernel(x_ref, mod_ref, w1_ref, b1_ref, w2_ref, b2_ref, ln_g_ref, ln_b_ref, o_ref,
                u_even, u_odd, x_even, x_odd, r_even, r_odd, h_buf, *, n_tiles, tiles_per_seq):
    D = D_MODEL
    s = pl.program_id(0)
    n_blocks = D_FF // FF_BLOCK
    rows_per_slice = FFN_TILE // n_blocks

    def batch_of(tile):
        return lax.div(jnp.clip(tile, 0, n_tiles - 1), tiles_per_seq)

    def pre(rs, u_new, x_new):
        mod_new = mod_ref[batch_of(s)]
        x = x_ref[rs, :]
        x_new[rs, :] = x
        u = _normalize(x) * (1.0 + mod_new[:, 4 * D:5 * D]) + mod_new[:, 3 * D:4 * D]
        u_new[rs, :] = u.astype(_BF16)
        return u

    def post(rs, r_old):
        y = _normalize(r_old[rs, :]) * ln_g_ref[...] + ln_b_ref[...]
        o_ref[rs, :] = y
        return y

    def step(u_new, u_mid, x_new, x_mid, r_old, r_mid):
        mod_mid = mod_ref[batch_of(s - 1)]
        for c in range(n_blocks):
            rs = slice(c * rows_per_slice, (c + 1) * rows_per_slice)
            token = _zero_token(post(rs, r_old)) + _zero_token(pre(rs, u_new, x_new))
            cs = slice(c * FF_BLOCK, (c + 1) * FF_BLOCK)
            bias = b1_ref[:, cs] + jnp.concatenate([token] * (FF_BLOCK // LANES), axis=1)
            h = jnp.maximum(_dot(u_mid[...], w1_ref[:, cs]) + bias, 0.0)
            h_buf[:, cs] = (h * h).astype(_BF16)
        g_f = mod_mid[:, 5 * D:6 * D]
        for n in range(D // COL_BLOCK):
            ns = slice(n * COL_BLOCK, (n + 1) * COL_BLOCK)
            ff = _dot(h_buf[...], w2_ref[:, ns]) + b2_ref[:, ns]
            r_mid[:, ns] = ALPHA * x_mid[:, ns] + g_f[:, ns] * ff

    all_rows = slice(0, FFN_TILE)
    in_steady_state = jnp.logical_and(s >= 1, s <= n_tiles)
    parity = lax.rem(s, 2)
    r_last = r_odd if (n_tiles - 1) % 2 else r_even

    @pl.when(s == 0)
    def _():
        r_odd[...] = jnp.zeros(r_odd.shape, _F32)
        pre(all_rows, u_even, x_even)

    @pl.when(jnp.logical_and(in_steady_state, parity == 0))
    def _():
        step(u_even, u_odd, x_even, x_odd, r_even, r_odd)

    @pl.when(jnp.logical_and(in_steady_state, parity == 1))
    def _():
        step(u_odd, u_even, x_odd, x_even, r_odd, r_even)

    @pl.when(s == n_tiles + 1)
    def _():
        post(all_rows, r_last)


def _tile_spec(tile, n_tiles, lag):
    return pl.BlockSpec((tile, D_MODEL), lambda s: (jnp.clip(s - lag, 0, n_tiles - 1), 0))


def _ffn_call(layer, x, mod, w1, b1, w2, b2, ln_g, ln_b):
    B, S, D = x.shape
    T = FFN_TILE
    n_tiles = B * S // T
    consts = (w1, b1, w2, b2, ln_g, ln_b)
    kernel_fn = functools.partial(_ffn_kernel, n_tiles=n_tiles, tiles_per_seq=S // T)
    out = pl.pallas_call(
        kernel_fn,
        grid=(n_tiles + 2,),
        in_specs=[_tile_spec(T, n_tiles, 0),
                  pl.BlockSpec((None,) + mod.shape[1:], lambda s: (layer, 0, 0, 0))]
                 + [_layer_spec(a, layer) for a in consts],
        out_specs=_tile_spec(T, n_tiles, 2),
        out_shape=jax.ShapeDtypeStruct((B * S, D), _F32),
        scratch_shapes=[pltpu.VMEM((T, D), _BF16), pltpu.VMEM((T, D), _BF16),
                        pltpu.VMEM((T, D), _F32), pltpu.VMEM((T, D), _F32),
                        pltpu.VMEM((T, D), _F32), pltpu.VMEM((T, D), _F32),
                        pltpu.VMEM((T, D_FF), _BF16)],
        compiler_params=pltpu.CompilerParams(
            dimension_semantics=("arbitrary",), vmem_limit_bytes=FFN_VMEM_LIMIT),
        name="relu2_mlp",
    )(x.reshape(B * S, D), mod, *consts)
    return out.reshape(B, S, D)


def _bias_pairs(rel_bias):
    period = BAND + CHUNK - 1
    m = np.arange(period)
    ext_idx = np.clip(BAND - 1 - m, -REL_CLIP, REL_CLIP) + REL_CLIP
    ext = rel_bias[:, :, ext_idx] * LOG2_E
    lead = ext.shape[:-1]
    hankel = jnp.tile(ext, (1, 1, CHUNK + 1))[..., :CHUNK * (period + 1)]
    hankel = hankel.reshape(*lead, CHUNK, period + 1)[..., :BAND]
    bias = hankel[..., ::-1, :]
    return bias.reshape(lead[0], HEAD_PAIRS, 2 * CHUNK, BAND)


def _block_diag(w):
    depth, groups, gin, gout = w.shape
    eye = jnp.eye(groups, dtype=w.dtype)[None, :, None, :, None]
    return (w[:, :, :, None, :] * eye).reshape(depth, groups * gin, groups * gout)


def kernel(x, c, w_ada, b_ada, w_in, b_gate, w_pool, pool_scale, rel_bias, conv_w, conv_b, conv_ln_g, conv_ln_b, w_br_pool, w_br_attn, w_br_conv, w_o, ln_mix_g, ln_mix_b, w_ff1, b_ff1, w_ff2, b_ff2, ln_ff_g, ln_ff_b):
    depth = w_in.shape[0]
    bf16 = lambda a: a.astype(_BF16)
    rows = lambda a: a.reshape(depth, 1, -1)
    mod = _ada_call(c, w_ada, b_ada)[:, :, None, :]
    mixer_params = (bf16(w_in), rows(b_gate), bf16(_block_diag(w_pool)), rows(pool_scale),
                    _bias_pairs(rel_bias), conv_w, rows(conv_b), rows(conv_ln_g), rows(conv_ln_b),
                    bf16(w_br_pool), bf16(w_br_attn), bf16(w_br_conv), bf16(w_o),
                    rows(ln_mix_g), rows(ln_mix_b))
    ffn_params = (bf16(w_ff1), rows(b_ff1), bf16(w_ff2), rows(b_ff2), rows(ln_ff_g), rows(ln_ff_b))
    for layer in range(depth):
        x = _mixer_call(layer, x, mod, *mixer_params)
        x = _ffn_call(layer, x, mod, *ffn_params)
    return x
```

```python
import functools

import jax
import jax.numpy as jnp
import numpy as np
from jax import lax
from jax.experimental import pallas as pl
from jax.experimental.pallas import tpu as pltpu

D_MODEL = 1024
DEPTH = 2
CHUNK = 64
POOL_WINDOWS = (2, 4, 8, 16)
POOL_GROUP = 64
D_POOL = POOL_GROUP * len(POOL_WINDOWS)
N_HEADS = 8
HEAD_DIM = 64
D_ATTN = N_HEADS * HEAD_DIM
N_PREV_CHUNKS = 8
REL_CLIP = 128
D_CONV = 256
CONV_WIDTH = 31
D_FF = 4 * D_MODEL
N_BRANCH = 3
ALPHA = (2.0 * DEPTH) ** 0.25
LN_EPS = 1e-5
NEG_INF = -1e30
LOG2_E = 1.4426950408889634

OFF_POOL = 0
OFF_Q = OFF_POOL + D_POOL
OFF_K = OFF_Q + D_ATTN
OFF_V = OFF_K + D_ATTN
OFF_CONV = OFF_V + D_ATTN
OFF_GATE = OFF_CONV + 2 * D_CONV
D_IN = OFF_GATE + N_BRANCH * D_MODEL

LANES = 128
SUBLANES = 8
MIX_TILE = 512
CHUNKS_PER_TILE = MIX_TILE // CHUNK
BAND_CHUNKS = N_PREV_CHUNKS + 2
BAND = BAND_CHUNKS * CHUNK
KV_HIST = (BAND_CHUNKS - 1) * CHUNK
CONV_HIST = 32
POOL_HIST = 16
HEAD_PAIRS = N_HEADS // 2
SCORE_SLOTS = 2 * HEAD_PAIRS
COL_BLOCK = 256
FFN_TILE = 512
FF_BLOCK = 512
ADA_BLOCK = 1536
MIX_VMEM_LIMIT = 52 * 1024 * 1024
FFN_VMEM_LIMIT = 48 * 1024 * 1024
ADA_VMEM_LIMIT = 32 * 1024 * 1024

_F32 = jnp.float32
_BF16 = jnp.bfloat16


def _dot(a, b):
    return jnp.dot(a, b, preferred_element_type=_F32)


def _normalize(x):
    mu = jnp.mean(x, axis=-1, keepdims=True)
    xc = x - mu
    var = jnp.mean(xc * xc, axis=-1, keepdims=True)
    return xc * lax.rsqrt(var + LN_EPS)


def _zero_token(v):
    rows, cols = v.shape
    folded = v.reshape(rows // SUBLANES, SUBLANES, cols).sum(axis=0)
    folded = sum(folded[:, k * LANES:(k + 1) * LANES] for k in range(cols // LANES))
    bits = lax.bitcast_convert_type(folded, jnp.uint32)
    half = jnp.uint32(16)
    zero = lax.shift_right_logical(lax.shift_right_logical(bits, half), half)
    return lax.bitcast_convert_type(zero, _F32)[0:1, :]


def _ada_kernel(c_ref, w_ref, b_ref, o_ref):
    c = c_ref[...]
    ca = (c * jax.nn.sigmoid(c)).astype(_BF16)
    o_ref[...] = _dot(ca, w_ref[...].astype(_BF16)) + b_ref[...]


def _ada_call(c, w_ada, b_ada):
    depth, d, n = w_ada.shape
    batch = c.shape[0]
    return pl.pallas_call(
        _ada_kernel,
        grid=(depth, n // ADA_BLOCK),
        in_specs=[
            pl.BlockSpec((batch, d), lambda l, j: (0, 0)),
            pl.BlockSpec((None, d, ADA_BLOCK), lambda l, j: (l, 0, j)),
            pl.BlockSpec((None, 1, ADA_BLOCK), lambda l, j: (l, 0, j)),
        ],
        out_specs=pl.BlockSpec((None, batch, ADA_BLOCK), lambda l, j: (l, 0, j)),
        out_shape=jax.ShapeDtypeStruct((depth, batch, n), _F32),
        compiler_params=pltpu.CompilerParams(
            dimension_semantics=("arbitrary", "arbitrary"), vmem_limit_bytes=ADA_VMEM_LIMIT),
        name="ada_mod",
    )(c, w_ada, b_ada.reshape(depth, 1, n))


def _mixer_kernel(x_ref, mod_ref, w_in_ref, b_gate_ref, w_pool_ref, pool_scale_ref, bias_ref,
                  conv_w_ref, conv_b_ref, conv_g_ref, conv_beta_ref, w_bp_ref, w_ba_ref, w_bc_ref,
                  w_o_ref, ln_g_ref, ln_b_ref, o_ref,
                  u_buf, qe_buf, qo_buf, k_buf, v_buf, attn_buf, pool_buf, conv_buf, conv_shift,
                  score_buf, rowmax_buf, prob_buf, gate_buf, merged_buf):
    T, D = MIX_TILE, D_MODEL
    i = pl.program_id(1)

    @pl.when(i == 0)
    def _():
        k_buf[0:KV_HIST, :] = jnp.zeros((KV_HIST, D_ATTN), _BF16)
        for p in range(HEAD_PAIRS):
            v_buf[0:KV_HIST, 2 * p * LANES:(2 * p + 1) * LANES] = jnp.zeros((KV_HIST, LANES), _BF16)
            v_buf[:, (2 * p + 1) * LANES:2 * (p + 1) * LANES] = jnp.ones((KV_HIST + MIX_TILE, LANES), _BF16)
        pool_buf[0:POOL_HIST, :] = jnp.zeros((POOL_HIST, D_POOL), _F32)
        conv_buf[0:CONV_HIST, :] = jnp.zeros((CONV_HIST, D_CONV), _F32)

    sh_m = mod_ref[:, 0:D]
    sc_m = mod_ref[:, D:2 * D]
    g_m = mod_ref[:, 2 * D:3 * D]
    halves = (slice(0, T // 2), slice(T // 2, T))

    def in_proj(off, width, rows=slice(0, T)):
        return _dot(u_buf[rows, :], w_in_ref[:, off:off + width])

    lane = lax.broadcasted_iota(jnp.int32, (CHUNK, LANES), 1)
    key_idx = lax.broadcasted_iota(jnp.int32, (1, BAND), 1)
    dyn0 = jnp.minimum(i, 0)

    def scores(j):
        chunk = i * CHUNKS_PER_TILE + j
        first_valid = jnp.maximum(BAND_CHUNKS - 1 - chunk, 1) * CHUNK
        mask_add = jnp.where(key_idx >= first_valid, 0.0, NEG_INF)
        rows = slice(j * CHUNK, (j + 1) * CHUNK)
        band = slice(j * CHUNK, j * CHUNK + BAND)
        for p in range(HEAD_PAIRS):
            cs = slice(p * LANES, (p + 1) * LANES)
            slot = (j % 2) * HEAD_PAIRS + p + dyn0
            qs = jnp.concatenate([qe_buf[rows, cs], qo_buf[rows, cs]], axis=0)
            sc = lax.dot_general(qs, k_buf[band, cs], (((1,), (1,)), ((), ())),
                                 preferred_element_type=_F32) + bias_ref[p] + mask_add
            score_buf[slot] = sc
            rowmax_buf[slot] = jnp.max(sc, axis=-1, keepdims=True)

    def softmax(j):
        for p in range(HEAD_PAIRS):
            slot = (j % 2) * HEAD_PAIRS + p + dyn0
            prob_buf[slot] = jnp.exp2(score_buf[slot] - rowmax_buf[slot]).astype(_BF16)

    def weighted_values(j):
        rows = slice(j * CHUNK, (j + 1) * CHUNK)
        band = slice(j * CHUNK, j * CHUNK + BAND)
        for p in range(HEAD_PAIRS):
            slot = (j % 2) * HEAD_PAIRS + p + dyn0
            pv = _dot(prob_buf[slot], v_buf[band, 2 * p * LANES:2 * (p + 1) * LANES])
            pv = pv[:, 0:LANES] * (1.0 / pv[:, LANES:2 * LANES])
            attn_buf[rows, p * LANES:(p + 1) * LANES] = jnp.where(
                lane < HEAD_DIM, pv[0:CHUNK], pv[CHUNK:2 * CHUNK]).astype(_BF16)

    def gates(first, count):
        for g in range(first, first + count):
            cs = slice(g * COL_BLOCK, (g + 1) * COL_BLOCK)
            gate_buf[:, cs] = jax.nn.sigmoid(in_proj(OFF_GATE + g * COL_BLOCK, COL_BLOCK) + b_gate_ref[:, cs])

    even_head = (lax.broadcasted_iota(jnp.int32, (T // 2, D_ATTN), 1) & (LANES - 1)) < HEAD_DIM
    for rows in halves:
        u_buf[rows, :] = (_normalize(x_ref[rows, :]) * (1.0 + sc_m) + sh_m).astype(_BF16)
        q = in_proj(OFF_Q, D_ATTN, rows) * (HEAD_DIM ** -0.5 * LOG2_E)
        qe_buf[rows, :] = jnp.where(even_head, q, 0.0).astype(_BF16)
        qo_buf[rows, :] = jnp.where(even_head, 0.0, q).astype(_BF16)
        kv_rows = slice(KV_HIST + rows.start, KV_HIST + rows.stop)
        k_buf[kv_rows, :] = in_proj(OFF_K, D_ATTN, rows).astype(_BF16)
        v = in_proj(OFF_V, D_ATTN, rows).astype(_BF16)
        for p in range(HEAD_PAIRS):
            v_buf[kv_rows, 2 * p * LANES:(2 * p + 1) * LANES] = v[:, p * LANES:(p + 1) * LANES]
    scores(0)

    pool_buf[POOL_HIST:POOL_HIST + T, :] = in_proj(OFF_POOL, D_POOL)
    zc = in_proj(OFF_CONV, 2 * D_CONV)
    conv_buf[CONV_HIST:CONV_HIST + T, :] = zc[:, 0:D_CONV] * jax.nn.sigmoid(zc[:, D_CONV:2 * D_CONV])
    softmax(0)

    t_abs = i * T + lax.broadcasted_iota(jnp.int32, (T, LANES), 0)
    upper_half = lax.broadcasted_iota(jnp.int32, (T, LANES), 1) >= POOL_GROUP
    pooled = []
    for col, (w_lo, w_hi) in enumerate(((POOL_WINDOWS[0], POOL_WINDOWS[1]), (POOL_WINDOWS[2], POOL_WINDOWS[3]))):
        cs = slice(col * LANES, (col + 1) * LANES)
        x0 = pool_buf[POOL_HIST:POOL_HIST + T, cs]
        acc = x0
        for sh in range(1, w_hi):
            xs = pool_buf[POOL_HIST - sh:POOL_HIST - sh + T, cs]
            acc = acc + (xs if sh < w_lo else jnp.where(upper_half, xs, 0.0))
        window = jnp.where(upper_half, w_hi, w_lo)
        count = jnp.minimum(t_abs + 1, window).astype(_F32)
        pooled.append(acc / count - x0)
    pooled = jnp.concatenate(pooled, axis=1).astype(_BF16)
    pool_feat = (_dot(pooled, w_pool_ref[...]) * pool_scale_ref[...]).astype(_BF16)

    gate_blocks = N_BRANCH * D // COL_BLOCK
    for j in range(CHUNKS_PER_TILE):
        if j + 1 < CHUNKS_PER_TILE:
            scores(j + 1)
        weighted_values(j)
        first = j * gate_blocks // CHUNKS_PER_TILE
        gates(first, (j + 1) * gate_blocks // CHUNKS_PER_TILE - first)
        if j + 1 < CHUNKS_PER_TILE:
            softmax(j + 1)

    span = CONV_HIST + T - SUBLANES
    for r in range(1, SUBLANES):
        conv_shift[r - 1, 0:span, :] = conv_buf[r:r + span, :]
    acc = None
    for tap in range(CONV_WIDTH):
        start = CONV_HIST - (CONV_WIDTH - 1) + tap
        base, r = (start // SUBLANES) * SUBLANES, start % SUBLANES
        rows = conv_buf[base:base + T, :] if r == 0 else conv_shift[r - 1, base:base + T, :]
        term = conv_w_ref[tap:tap + 1, :] * rows
        acc = term if acc is None else acc + term
    hc = _normalize(acc + conv_b_ref[...]) * conv_g_ref[...] + conv_beta_ref[...]
    conv_feat = (hc * jax.nn.sigmoid(hc)).astype(_BF16)

    for cch in range(KV_HIST // CHUNK):
        dst = slice(cch * CHUNK, (cch + 1) * CHUNK)
        src = slice(T + cch * CHUNK, T + (cch + 1) * CHUNK)
        k_buf[dst, :] = k_buf[src, :]
        for p in range(HEAD_PAIRS):
            vs = slice(2 * p * LANES, (2 * p + 1) * LANES)
            v_buf[dst, vs] = v_buf[src, vs]
    pool_buf[0:POOL_HIST, :] = pool_buf[T:T + POOL_HIST, :]
    conv_buf[0:CONV_HIST, :] = conv_buf[T:T + CONV_HIST, :]

    feats = (pool_feat, attn_buf[...], conv_feat)
    weights = (w_bp_ref, w_ba_ref, w_bc_ref)
    col_blocks = D // COL_BLOCK
    for nb in range(col_blocks):
        cs = slice(nb * COL_BLOCK, (nb + 1) * COL_BLOCK)
        merged = None
        for br in range(N_BRANCH):
            gs = slice(br * D + nb * COL_BLOCK, br * D + (nb + 1) * COL_BLOCK)
            term = gate_buf[:, gs] * _dot(feats[br], weights[br][:, cs])
            merged = term if merged is None else merged + term
        merged_buf[:, cs] = merged.astype(_BF16)

    token = None
    for rows in halves:
        gain = g_m if token is None else g_m + jnp.concatenate([token] * (D // LANES), axis=1)
        r = ALPHA * x_ref[rows, :] + gain * _dot(merged_buf[rows, :], w_o_ref[...])
        y = _normalize(r) * ln_g_ref[...] + ln_b_ref[...]
        o_ref[rows, :] = y
        token = _zero_token(y)


def _layer_spec(stacked, layer):
    index = (layer,) + (0,) * (stacked.ndim - 1)
    return pl.BlockSpec((None,) + stacked.shape[1:], lambda *_: index, pipeline_mode=pl.Buffered(1))


def _mixer_call(layer, x, mod, w_in, b_gate, w_pool_bd, pool_scale, bias_pairs, conv_w, conv_b,
                conv_g, conv_beta, w_bp, w_ba, w_bc, w_o, ln_g, ln_b):
    B, S, D = x.shape
    T = MIX_TILE
    consts = (w_in, b_gate, w_pool_bd, pool_scale, bias_pairs, conv_w, conv_b, conv_g, conv_beta,
              w_bp, w_ba, w_bc, w_o, ln_g, ln_b)
    return pl.pallas_call(
        _mixer_kernel,
        grid=(B, S // T),
        in_specs=[pl.BlockSpec((None, T, D), lambda b, i: (b, i, 0)),
                  pl.BlockSpec((None, None, 1, mod.shape[-1]), lambda b, i: (layer, b, 0, 0))]
                 + [_layer_spec(a, layer) for a in consts],
        out_specs=pl.BlockSpec((None, T, D), lambda b, i: (b, i, 0)),
        out_shape=jax.ShapeDtypeStruct((B, S, D), _F32),
        scratch_shapes=[
            pltpu.VMEM((T, D), _BF16),
            pltpu.VMEM((T, D_ATTN), _BF16),
            pltpu.VMEM((T, D_ATTN), _BF16),
            pltpu.VMEM((KV_HIST + T, D_ATTN), _BF16),
            pltpu.VMEM((KV_HIST + T, 2 * D_ATTN), _BF16),
            pltpu.VMEM((T, D_ATTN), _BF16),
            pltpu.VMEM((POOL_HIST + T, D_POOL), _F32),
            pltpu.VMEM((CONV_HIST + T, D_CONV), _F32),
            pltpu.VMEM((SUBLANES - 1, CONV_HIST + T, D_CONV), _F32),
            pltpu.VMEM((SCORE_SLOTS, 2 * CHUNK, BAND), _F32),
            pltpu.VMEM((SCORE_SLOTS, 2 * CHUNK, 1), _F32),
            pltpu.VMEM((SCORE_SLOTS, 2 * CHUNK, BAND), _BF16),
            pltpu.VMEM((T, N_BRANCH * D), _F32),
            pltpu.VMEM((T, D), _BF16),
        ],
        compiler_params=pltpu.CompilerParams(
            dimension_semantics=("arbitrary", "arbitrary"), vmem_limit_bytes=MIX_VMEM_LIMIT),
        name="token_mixer",
    )(x, mod, *consts)


def _ffn_kernel(x_ref, mod_ref, w1_ref, b1_ref, w2_ref, b2_ref, ln_g_ref, ln_b_ref, o_ref,
                u_even, u_odd, x_even, x_odd, r_even, r_odd, h_buf, *, n_tiles, tiles_per_seq):
    D = D_MODEL
    s = pl.program_id(0)
    n_blocks = D_FF // FF_BLOCK
    rows_per_slice = FFN_TILE // n_blocks

    def batch_of(tile):
        return lax.div(jnp.clip(tile, 0, n_tiles - 1), tiles_per_seq)

    def pre(rs, u_new, x_new):
        mod_new = mod_ref[batch_of(s)]
        x = x_ref[rs, :]
        x_new[rs, :] = x
        u = _normalize(x) * (1.0 + mod_new[:, 4 * D:5 * D]) + mod_new[:, 3 * D:4 * D]
        u_new[rs, :] = u.astype(_BF16)
        return u

    def post(rs, r_old):
        y = _normalize(r_old[rs, :]) * ln_g_ref[...] + ln_b_ref[...]
        o_ref[rs, :] = y
        return y

    def step(u_new, u_mid, x_new, x_mid, r_old, r_mid):
        mod_mid = mod_ref[batch_of(s - 1)]
        for c in range(n_blocks):
            rs = slice(c * rows_per_slice, (c + 1) * rows_per_slice)
            token = _zero_token(post(rs, r_old)) + _zero_token(pre(rs, u_new, x_new))
            cs = slice(c * FF_BLOCK, (c + 1) * FF_BLOCK)
            bias = b1_ref[:, cs] + jnp.concatenate([token] * (FF_BLOCK // LANES), axis=1)
            h = jnp.maximum(_dot(u_mid[...], w1_ref[:, cs]) + bias, 0.0)
            h_buf[:, cs] = (h * h).astype(_BF16)
        g_f = mod_mid[:, 5 * D:6 * D]
        for n in range(D // COL_BLOCK):
            ns = slice(n * COL_BLOCK, (n + 1) * COL_BLOCK)
            ff = _dot(h_buf[...], w2_ref[:, ns]) + b2_ref[:, ns]
            r_mid[:, ns] = ALPHA * x_mid[:, ns] + g_f[:, ns] * ff

    all_rows = slice(0, FFN_TILE)
    in_steady_state = jnp.logical_and(s >= 1, s <= n_tiles)
    parity = lax.rem(s, 2)
    r_last = r_odd if (n_tiles - 1) % 2 else r_even

    @pl.when(s == 0)
    def _():
        r_odd[...] = jnp.zeros(r_odd.shape, _F32)
        pre(all_rows, u_even, x_even)

    @pl.when(jnp.logical_and(in_steady_state, parity == 0))
    def _():
        step(u_even, u_odd, x_even, x_odd, r_even, r_odd)

    @pl.when(jnp.logical_and(in_steady_state, parity == 1))
    def _():
        step(u_odd, u_even, x_odd, x_even, r_odd, r_even)

    @pl.when(s == n_tiles + 1)
    def _():
        post(all_rows, r_last)


def _tile_spec(tile, n_tiles, lag):
    return pl.BlockSpec((tile, D_MODEL), lambda s: (jnp.clip(s - lag, 0, n_tiles - 1), 0))


def _ffn_call(layer, x, mod, w1, b1, w2, b2, ln_g, ln_b):
    B, S, D = x.shape
    T = FFN_TILE
    n_tiles = B * S // T
    consts = (w1, b1, w2, b2, ln_g, ln_b)
    kernel_fn = functools.partial(_ffn_kernel, n_tiles=n_tiles, tiles_per_seq=S // T)
    out = pl.pallas_call(
        kernel_fn,
        grid=(n_tiles + 2,),
        in_specs=[_tile_spec(T, n_tiles, 0),
                  pl.BlockSpec((None,) + mod.shape[1:], lambda s: (layer, 0, 0, 0))]
                 + [_layer_spec(a, layer) for a in consts],
        out_specs=_tile_spec(T, n_tiles, 2),
        out_shape=jax.ShapeDtypeStruct((B * S, D), _F32),
        scratch_shapes=[pltpu.VMEM((T, D), _BF16), pltpu.VMEM((T, D), _BF16),
                        pltpu.VMEM((T, D), _F32), pltpu.VMEM((T, D), _F32),
                        pltpu.VMEM((T, D), _F32), pltpu.VMEM((T, D), _F32),
                        pltpu.VMEM((T, D_FF), _BF16)],
        compiler_params=pltpu.CompilerParams(
            dimension_semantics=("arbitrary",), vmem_limit_bytes=FFN_VMEM_LIMIT),
        name="relu2_mlp",
    )(x.reshape(B * S, D), mod, *consts)
    return out.reshape(B, S, D)


def _bias_pairs(rel_bias):
    period = BAND + CHUNK - 1
    m = np.arange(period)
    ext_idx = np.clip(BAND - 1 - m, -REL_CLIP, REL_CLIP) + REL_CLIP
    ext = rel_bias[:, :, ext_idx] * LOG2_E
    lead = ext.shape[:-1]
    hankel = jnp.tile(ext, (1, 1, CHUNK + 1))[..., :CHUNK * (period + 1)]
    hankel = hankel.reshape(*lead, CHUNK, period + 1)[..., :BAND]
    bias = hankel[..., ::-1, :]
    return bias.reshape(lead[0], HEAD_PAIRS, 2 * CHUNK, BAND)


def _block_diag(w):
    depth, groups, gin, gout = w.shape
    eye = jnp.eye(groups, dtype=w.dtype)[None, :, None, :, None]
    return (w[:, :, :, None, :] * eye).reshape(depth, groups * gin, groups * gout)


def kernel(x, c, w_ada, b_ada, w_in, b_gate, w_pool, pool_scale, rel_bias, conv_w, conv_b, conv_ln_g, conv_ln_b, w_br_pool, w_br_attn, w_br_conv, w_o, ln_mix_g, ln_mix_b, w_ff1, b_ff1, w_ff2, b_ff2, ln_ff_g, ln_ff_b):
    depth = w_in.shape[0]
    bf16 = lambda a: a.astype(_BF16)
    rows = lambda a: a.reshape(depth, 1, -1)
    mod = _ada_call(c, w_ada, b_ada)[:, :, None, :]
    mixer_params = (bf16(w_in), rows(b_gate), bf16(_block_diag(w_pool)), rows(pool_scale),
                    _bias_pairs(rel_bias), conv_w, rows(conv_b), rows(conv_ln_g), rows(conv_ln_b),
                    bf16(w_br_pool), bf16(w_br_attn), bf16(w_br_conv), bf16(w_o),
                    rows(ln_mix_g), rows(ln_mix_b))
    ffn_params = (bf16(w_ff1), rows(b_ff1), bf16(w_ff2), rows(b_ff2), rows(ln_ff_g), rows(ln_ff_b))
    for layer in range(depth):
        x = _mixer_call(layer, x, mod, *mixer_params)
        x = _ffn_call(layer, x, mod, *ffn_params)
    return x
```

```python
import functools

import jax
import jax.numpy as jnp
import numpy as np
from jax import lax
from jax.experimental import pallas as pl
from jax.experimental.pallas import tpu as pltpu

D_MODEL = 1024
DEPTH = 2
CHUNK = 64
POOL_WINDOWS = (2, 4, 8, 16)
POOL_GROUP = 64
D_POOL = POOL_GROUP * len(POOL_WINDOWS)
N_HEADS = 8
HEAD_DIM = 64
D_ATTN = N_HEADS * HEAD_DIM
N_PREV_CHUNKS = 8
REL_CLIP = 128
D_CONV = 256
CONV_WIDTH = 31
D_FF = 4 * D_MODEL
N_BRANCH = 3
ALPHA = (2.0 * DEPTH) ** 0.25
LN_EPS = 1e-5
NEG_INF = -1e30
LOG2_E = 1.4426950408889634

OFF_POOL = 0
OFF_Q = OFF_POOL + D_POOL
OFF_K = OFF_Q + D_ATTN
OFF_V = OFF_K + D_ATTN
OFF_CONV = OFF_V + D_ATTN
OFF_GATE = OFF_CONV + 2 * D_CONV
D_IN = OFF_GATE + N_BRANCH * D_MODEL

LANES = 128
SUBLANES = 8
MIX_TILE = 512
CHUNKS_PER_TILE = MIX_TILE // CHUNK
BAND_CHUNKS = N_PREV_CHUNKS + 2
BAND = BAND_CHUNKS * CHUNK
KV_HIST = (BAND_CHUNKS - 1) * CHUNK
CONV_HIST = 32
POOL_HIST = 16
HEAD_PAIRS = N_HEADS // 2
SCORE_SLOTS = 2 * HEAD_PAIRS
COL_BLOCK = 256
FFN_TILE = 512
FF_BLOCK = 512
ADA_BLOCK = 1536
MIX_VMEM_LIMIT = 52 * 1024 * 1024
FFN_VMEM_LIMIT = 48 * 1024 * 1024
ADA_VMEM_LIMIT = 32 * 1024 * 1024

_F32 = jnp.float32
_BF16 = jnp.bfloat16


def _dot(a, b):
    return jnp.dot(a, b, preferred_element_type=_F32)


def _normalize(x):
    mu = jnp.mean(x, axis=-1, keepdims=True)
    xc = x - mu
    var = jnp.mean(xc * xc, axis=-1, keepdims=True)
    return xc * lax.rsqrt(var + LN_EPS)


def _zero_token(v):
    rows, cols = v.shape
    folded = v.reshape(rows // SUBLANES, SUBLANES, cols).sum(axis=0)
    folded = sum(folded[:, k * LANES:(k + 1) * LANES] for k in range(cols // LANES))
    bits = lax.bitcast_convert_type(folded, jnp.uint32)
    half = jnp.uint32(16)
    zero = lax.shift_right_logical(lax.shift_right_logical(bits, half), half)
    return lax.bitcast_convert_type(zero, _F32)[0:1, :]


def _ada_kernel(c_ref, w_ref, b_ref, o_ref):
    c = c_ref[...]
    ca = (c * jax.nn.sigmoid(c)).astype(_BF16)
    o_ref[...] = _dot(ca, w_ref[...].astype(_BF16)) + b_ref[...]


def _ada_call(c, w_ada, b_ada):
    depth, d, n = w_ada.shape
    batch = c.shape[0]
    return pl.pallas_call(
        _ada_kernel,
        grid=(depth, n // ADA_BLOCK),
        in_specs=[
            pl.BlockSpec((batch, d), lambda l, j: (0, 0)),
            pl.BlockSpec((None, d, ADA_BLOCK), lambda l, j: (l, 0, j)),
            pl.BlockSpec((None, 1, ADA_BLOCK), lambda l, j: (l, 0, j)),
        ],
        out_specs=pl.BlockSpec((None, batch, ADA_BLOCK), lambda l, j: (l, 0, j)),
        out_shape=jax.ShapeDtypeStruct((depth, batch, n), _F32),
        compiler_params=pltpu.CompilerParams(
            dimension_semantics=("arbitrary", "arbitrary"), vmem_limit_bytes=ADA_VMEM_LIMIT),
        name="ada_mod",
    )(c, w_ada, b_ada.reshape(depth, 1, n))


def _mixer_kernel(x_ref, mod_ref, w_in_ref, b_gate_ref, w_pool_ref, pool_scale_ref, bias_ref,
                  conv_w_ref, conv_b_ref, conv_g_ref, conv_beta_ref, w_bp_ref, w_ba_ref, w_bc_ref,
                  w_o_ref, ln_g_ref, ln_b_ref, o_ref,
                  u_buf, qe_buf, qo_buf, k_buf, v_buf, attn_buf, pool_buf, conv_buf, conv_shift,
                  score_buf, prob_buf, rdenom_buf, gate_buf, merged_buf):
    T, D = MIX_TILE, D_MODEL
    i = pl.program_id(1)

    @pl.when(i == 0)
    def _():
        k_buf[0:KV_HIST, :] = jnp.zeros((KV_HIST, D_ATTN), _BF16)
        v_buf[0:KV_HIST, :] = jnp.zeros((KV_HIST, D_ATTN), _BF16)
        pool_buf[0:POOL_HIST, :] = jnp.zeros((POOL_HIST, D_POOL), _F32)
        conv_buf[0:CONV_HIST, :] = jnp.zeros((CONV_HIST, D_CONV), _F32)

    sh_m = mod_ref[:, 0:D]
    sc_m = mod_ref[:, D:2 * D]
    g_m = mod_ref[:, 2 * D:3 * D]
    halves = (slice(0, T // 2), slice(T // 2, T))

    def in_proj(off, width, rows=slice(0, T)):
        return _dot(u_buf[rows, :], w_in_ref[:, off:off + width])

    lane = lax.broadcasted_iota(jnp.int32, (CHUNK, LANES), 1)
    key_idx = lax.broadcasted_iota(jnp.int32, (1, BAND), 1)
    dyn0 = jnp.minimum(i, 0)

    def scores(j):
        rows = slice(j * CHUNK, (j + 1) * CHUNK)
        band = slice(j * CHUNK, j * CHUNK + BAND)
        for p in range(HEAD_PAIRS):
            cs = slice(p * LANES, (p + 1) * LANES)
            qs = jnp.concatenate([qe_buf[rows, cs], qo_buf[rows, cs]], axis=0)
            score_buf[(j % 2) * HEAD_PAIRS + p + dyn0] = lax.dot_general(
                qs, k_buf[band, cs], (((1,), (1,)), ((), ())), preferred_element_type=_F32)

    def softmax(j):
        chunk = i * CHUNKS_PER_TILE + j
        first_valid = jnp.maximum(BAND_CHUNKS - 1 - chunk, 1) * CHUNK
        mask_add = jnp.where(key_idx >= first_valid, 0.0, NEG_INF)
        for p in range(HEAD_PAIRS):
            slot = (j % 2) * HEAD_PAIRS + p + dyn0
            sc = score_buf[slot] + bias_ref[p] + mask_add
            e = jnp.exp2(sc - jnp.max(sc, axis=-1, keepdims=True))
            rdenom_buf[slot] = 1.0 / jnp.sum(e, axis=-1, keepdims=True)
            prob_buf[slot] = e.astype(_BF16)

    def weighted_values(j):
        rows = slice(j * CHUNK, (j + 1) * CHUNK)
        band = slice(j * CHUNK, j * CHUNK + BAND)
        for p in range(HEAD_PAIRS):
            cs = slice(p * LANES, (p + 1) * LANES)
            slot = (j % 2) * HEAD_PAIRS + p + dyn0
            pv = _dot(prob_buf[slot], v_buf[band, cs]) * rdenom_buf[slot]
            attn_buf[rows, cs] = jnp.where(lane < HEAD_DIM, pv[0:CHUNK], pv[CHUNK:2 * CHUNK]).astype(_BF16)

    def gates(first, count):
        for g in range(first, first + count):
            cs = slice(g * COL_BLOCK, (g + 1) * COL_BLOCK)
            gate_buf[:, cs] = jax.nn.sigmoid(in_proj(OFF_GATE + g * COL_BLOCK, COL_BLOCK) + b_gate_ref[:, cs])

    even_head = (lax.broadcasted_iota(jnp.int32, (T // 2, D_ATTN), 1) & (LANES - 1)) < HEAD_DIM
    for rows in halves:
        u_buf[rows, :] = (_normalize(x_ref[rows, :]) * (1.0 + sc_m) + sh_m).astype(_BF16)
        q = in_proj(OFF_Q, D_ATTN, rows) * (HEAD_DIM ** -0.5 * LOG2_E)
        qe_buf[rows, :] = jnp.where(even_head, q, 0.0).astype(_BF16)
        qo_buf[rows, :] = jnp.where(even_head, 0.0, q).astype(_BF16)
        kv_rows = slice(KV_HIST + rows.start, KV_HIST + rows.stop)
        k_buf[kv_rows, :] = in_proj(OFF_K, D_ATTN, rows).astype(_BF16)
        v_buf[kv_rows, :] = in_proj(OFF_V, D_ATTN, rows).astype(_BF16)
    scores(0)

    pool_buf[POOL_HIST:POOL_HIST + T, :] = in_proj(OFF_POOL, D_POOL)
    zc = in_proj(OFF_CONV, 2 * D_CONV)
    conv_buf[CONV_HIST:CONV_HIST + T, :] = zc[:, 0:D_CONV] * jax.nn.sigmoid(zc[:, D_CONV:2 * D_CONV])
    softmax(0)

    t_abs = i * T + lax.broadcasted_iota(jnp.int32, (T, LANES), 0)
    upper_half = lax.broadcasted_iota(jnp.int32, (T, LANES), 1) >= POOL_GROUP
    pooled = []
    for col, (w_lo, w_hi) in enumerate(((POOL_WINDOWS[0], POOL_WINDOWS[1]), (POOL_WINDOWS[2], POOL_WINDOWS[3]))):
        cs = slice(col * LANES, (col + 1) * LANES)
        x0 = pool_buf[POOL_HIST:POOL_HIST + T, cs]
        acc = x0
        for sh in range(1, w_hi):
            xs = pool_buf[POOL_HIST - sh:POOL_HIST - sh + T, cs]
            acc = acc + (xs if sh < w_lo else jnp.where(upper_half, xs, 0.0))
        window = jnp.where(upper_half, w_hi, w_lo)
        count = jnp.minimum(t_abs + 1, window).astype(_F32)
        pooled.append(acc / count - x0)
    pooled = jnp.concatenate(pooled, axis=1).astype(_BF16)
    pool_feat = (_dot(pooled, w_pool_ref[...]) * pool_scale_ref[...]).astype(_BF16)

    gate_blocks = N_BRANCH * D // COL_BLOCK
    for j in range(CHUNKS_PER_TILE):
        if j + 1 < CHUNKS_PER_TILE:
            scores(j + 1)
        weighted_values(j)
        first = j * gate_blocks // CHUNKS_PER_TILE
        gates(first, (j + 1) * gate_blocks // CHUNKS_PER_TILE - first)
        if j + 1 < CHUNKS_PER_TILE:
            softmax(j + 1)

    span = CONV_HIST + T - SUBLANES
    for r in range(1, SUBLANES):
        conv_shift[r - 1, 0:span, :] = conv_buf[r:r + span, :]
    acc = None
    for tap in range(CONV_WIDTH):
        start = CONV_HIST - (CONV_WIDTH - 1) + tap
        base, r = (start // SUBLANES) * SUBLANES, start % SUBLANES
        rows = conv_buf[base:base + T, :] if r == 0 else conv_shift[r - 1, base:base + T, :]
        term = conv_w_ref[tap:tap + 1, :] * rows
        acc = term if acc is None else acc + term
    hc = _normalize(acc + conv_b_ref[...]) * conv_g_ref[...] + conv_beta_ref[...]
    conv_feat = (hc * jax.nn.sigmoid(hc)).astype(_BF16)

    for cch in range(KV_HIST // CHUNK):
        dst = slice(cch * CHUNK, (cch + 1) * CHUNK)
        src = slice(T + cch * CHUNK, T + (cch + 1) * CHUNK)
        k_buf[dst, :] = k_buf[src, :]
        v_buf[dst, :] = v_buf[src, :]
    pool_buf[0:POOL_HIST, :] = pool_buf[T:T + POOL_HIST, :]
    conv_buf[0:CONV_HIST, :] = conv_buf[T:T + CONV_HIST, :]

    feats = (pool_feat, attn_buf[...], conv_feat)
    weights = (w_bp_ref, w_ba_ref, w_bc_ref)
    col_blocks = D // COL_BLOCK
    for nb in range(col_blocks):
        cs = slice(nb * COL_BLOCK, (nb + 1) * COL_BLOCK)
        merged = None
        for br in range(N_BRANCH):
            gs = slice(br * D + nb * COL_BLOCK, br * D + (nb + 1) * COL_BLOCK)
            term = gate_buf[:, gs] * _dot(feats[br], weights[br][:, cs])
            merged = term if merged is None else merged + term
        merged_buf[:, cs] = merged.astype(_BF16)

    token = None
    for rows in halves:
        gain = g_m if token is None else g_m + jnp.concatenate([token] * (D // LANES), axis=1)
        r = ALPHA * x_ref[rows, :] + gain * _dot(merged_buf[rows, :], w_o_ref[...])
        y = _normalize(r) * ln_g_ref[...] + ln_b_ref[...]
        o_ref[rows, :] = y
        token = _zero_token(y)


def _layer_spec(stacked, layer):
    index = (layer,) + (0,) * (stacked.ndim - 1)
    return pl.BlockSpec((None,) + stacked.shape[1:], lambda *_: index, pipeline_mode=pl.Buffered(1))


def _mixer_call(layer, x, mod, w_in, b_gate, w_pool_bd, pool_scale, bias_pairs, conv_w, conv_b,
                conv_g, conv_beta, w_bp, w_ba, w_bc, w_o, ln_g, ln_b):
    B, S, D = x.shape
    T = MIX_TILE
    consts = (w_in, b_gate, w_pool_bd, pool_scale, bias_pairs, conv_w, conv_b, conv_g, conv_beta,
              w_bp, w_ba, w_bc, w_o, ln_g, ln_b)
    return pl.pallas_call(
        _mixer_kernel,
        grid=(B, S // T),
        in_specs=[pl.BlockSpec((None, T, D), lambda b, i: (b, i, 0)),
                  pl.BlockSpec((None, None, 1, mod.shape[-1]), lambda b, i: (layer, b, 0, 0))]
                 + [_layer_spec(a, layer) for a in consts],
        out_specs=pl.BlockSpec((None, T, D), lambda b, i: (b, i, 0)),
        out_shape=jax.ShapeDtypeStruct((B, S, D), _F32),
        scratch_shapes=[
            pltpu.VMEM((T, D), _BF16),
            pltpu.VMEM((T, D_ATTN), _BF16),
            pltpu.VMEM((T, D_ATTN), _BF16),
            pltpu.VMEM((KV_HIST + T, D_ATTN), _BF16),
            pltpu.VMEM((KV_HIST + T, D_ATTN), _BF16),
            pltpu.VMEM((T, D_ATTN), _BF16),
            pltpu.VMEM((POOL_HIST + T, D_POOL), _F32),
            pltpu.VMEM((CONV_HIST + T, D_CONV), _F32),
            pltpu.VMEM((SUBLANES - 1, CONV_HIST + T, D_CONV), _F32),
            pltpu.VMEM((SCORE_SLOTS, 2 * CHUNK, BAND), _F32),
            pltpu.VMEM((SCORE_SLOTS, 2 * CHUNK, BAND), _BF16),
            pltpu.VMEM((SCORE_SLOTS, 2 * CHUNK, 1), _F32),
            pltpu.VMEM((T, N_BRANCH * D), _F32),
            pltpu.VMEM((T, D), _BF16),
        ],
        compiler_params=pltpu.CompilerParams(
            dimension_semantics=("arbitrary", "arbitrary"), vmem_limit_bytes=MIX_VMEM_LIMIT),
        name="token_mixer",
    )(x, mod, *consts)


def _ffn_kernel(x_ref, mod_ref, w1_ref, b1_ref, w2_ref, b2_ref, ln_g_ref, ln_b_ref, o_ref,
                u_even, u_odd, x_even, x_odd, r_even, r_odd, h_buf, *, n_tiles, tiles_per_seq):
    D = D_MODEL
    s = pl.program_id(0)
    n_blocks = D_FF // FF_BLOCK
    rows_per_slice = FFN_TILE // n_blocks

    def batch_of(tile):
        return lax.div(jnp.clip(tile, 0, n_tiles - 1), tiles_per_seq)

    def pre(rs, u_new, x_new):
        mod_new = mod_ref[batch_of(s)]
        x = x_ref[rs, :]
        x_new[rs, :] = x
        u = _normalize(x) * (1.0 + mod_new[:, 4 * D:5 * D]) + mod_new[:, 3 * D:4 * D]
        u_new[rs, :] = u.astype(_BF16)
        return u

    def post(rs, r_old):
        y = _normalize(r_old[rs, :]) * ln_g_ref[...] + ln_b_ref[...]
        o_ref[rs, :] = y
        return y

    def step(u_new, u_mid, x_new, x_mid, r_old, r_mid):
        mod_mid = mod_ref[batch_of(s - 1)]
        for c in range(n_blocks):
            rs = slice(c * rows_per_slice, (c + 1) * rows_per_slice)
            token = _zero_token(post(rs, r_old)) + _zero_token(pre(rs, u_new, x_new))
            cs = slice(c * FF_BLOCK, (c + 1) * FF_BLOCK)
            bias = b1_ref[:, cs] + jnp.concatenate([token] * (FF_BLOCK // LANES), axis=1)
            h = jnp.maximum(_dot(u_mid[...], w1_ref[:, cs]) + bias, 0.0)
            h_buf[:, cs] = (h * h).astype(_BF16)
        g_f = mod_mid[:, 5 * D:6 * D]
        for n in range(D // COL_BLOCK):
            ns = slice(n * COL_BLOCK, (n + 1) * COL_BLOCK)
            ff = _dot(h_buf[...], w2_ref[:, ns]) + b2_ref[:, ns]
            r_mid[:, ns] = ALPHA * x_mid[:, ns] + g_f[:, ns] * ff

    all_rows = slice(0, FFN_TILE)
    in_steady_state = jnp.logical_and(s >= 1, s <= n_tiles)
    parity = lax.rem(s, 2)
    r_last = r_odd if (n_tiles - 1) % 2 else r_even

    @pl.when(s == 0)
    def _():
        r_odd[...] = jnp.zeros(r_odd.shape, _F32)
        pre(all_rows, u_even, x_even)

    @pl.when(jnp.logical_and(in_steady_state, parity == 0))
    def _():
        step(u_even, u_odd, x_even, x_odd, r_even, r_odd)

    @pl.when(jnp.logical_and(in_steady_state, parity == 1))
    def _():
        step(u_odd, u_even, x_odd, x_even, r_odd, r_even)

    @pl.when(s == n_tiles + 1)
    def _():
        post(all_rows, r_last)


def _tile_spec(tile, n_tiles, lag):
    return pl.BlockSpec((tile, D_MODEL), lambda s: (jnp.clip(s - lag, 0, n_tiles - 1), 0))


def _ffn_call(layer, x, mod, w1, b1, w2, b2, ln_g, ln_b):
    B, S, D = x.shape
    T = FFN_TILE
    n_tiles = B * S // T
    consts = (w1, b1, w2, b2, ln_g, ln_b)
    kernel_fn = functools.partial(_ffn_kernel, n_tiles=n_tiles, tiles_per_seq=S // T)
    out = pl.pallas_call(
        kernel_fn,
        grid=(n_tiles + 2,),
        in_specs=[_tile_spec(T, n_tiles, 0),
                  pl.BlockSpec((None,) + mod.shape[1:], lambda s: (layer, 0, 0, 0))]
                 + [_layer_spec(a, layer) for a in consts],
        out_specs=_tile_spec(T, n_tiles, 2),
        out_shape=jax.ShapeDtypeStruct((B * S, D), _F32),
        scratch_shapes=[pltpu.VMEM((T, D), _BF16), pltpu.VMEM((T, D), _BF16),
                        pltpu.VMEM((T, D), _F32), pltpu.VMEM((T, D), _F32),
                        pltpu.VMEM((T, D), _F32), pltpu.VMEM((T, D), _F32),
                        pltpu.VMEM((T, D_FF), _BF16)],
        compiler_params=pltpu.CompilerParams(
            dimension_semantics=("arbitrary",), vmem_limit_bytes=FFN_VMEM_LIMIT),
        name="relu2_mlp",
    )(x.reshape(B * S, D), mod, *consts)
    return out.reshape(B, S, D)


def _bias_pairs(rel_bias):
    period = BAND + CHUNK - 1
    m = np.arange(period)
    ext_idx = np.clip(BAND - 1 - m, -REL_CLIP, REL_CLIP) + REL_CLIP
    ext = rel_bias[:, :, ext_idx] * LOG2_E
    lead = ext.shape[:-1]
    hankel = jnp.tile(ext, (1, 1, CHUNK + 1))[..., :CHUNK * (period + 1)]
    hankel = hankel.reshape(*lead, CHUNK, period + 1)[..., :BAND]
    bias = hankel[..., ::-1, :]
    return bias.reshape(lead[0], HEAD_PAIRS, 2 * CHUNK, BAND)


def _block_diag(w):
    depth, groups, gin, gout = w.shape
    eye = jnp.eye(groups, dtype=w.dtype)[None, :, None, :, None]
    return (w[:, :, :, None, :] * eye).reshape(depth, groups * gin, groups * gout)


def kernel(x, c, w_ada, b_ada, w_in, b_gate, w_pool, pool_scale, rel_bias, conv_w, conv_b, conv_ln_g, conv_ln_b, w_br_pool, w_br_attn, w_br_conv, w_o, ln_mix_g, ln_mix_b, w_ff1, b_ff1, w_ff2, b_ff2, ln_ff_g, ln_ff_b):
    depth = w_in.shape[0]
    bf16 = lambda a: a.astype(_BF16)
    rows = lambda a: a.reshape(depth, 1, -1)
    mod = _ada_call(c, w_ada, b_ada)[:, :, None, :]
    mixer_params = (bf16(w_in), rows(b_gate), bf16(_block_diag(w_pool)), rows(pool_scale),
                    _bias_pairs(rel_bias), conv_w, rows(conv_b), rows(conv_ln_g), rows(conv_ln_b),
                    bf16(w_br_pool), bf16(w_br_attn), bf16(w_br_conv), bf16(w_o),
                    rows(ln_mix_g), rows(ln_mix_b))
    ffn_params = (bf16(w_ff1), rows(b_ff1), bf16(w_ff2), rows(b_ff2), rows(ln_ff_g), rows(ln_ff_b))
    for layer in range(depth):
        x = _mixer_call(layer, x, mod, *mixer_params)
        x = _ffn_call(layer, x, mod, *ffn_params)
    return x
```

```python
import functools

import jax
import jax.numpy as jnp
import numpy as np
from jax import lax
from jax.experimental import pallas as pl
from jax.experimental.pallas import tpu as pltpu

D_MODEL = 1024
DEPTH = 2
CHUNK = 64
POOL_WINDOWS = (2, 4, 8, 16)
POOL_GROUP = 64
D_POOL = POOL_GROUP * len(POOL_WINDOWS)
N_HEADS = 8
HEAD_DIM = 64
D_ATTN = N_HEADS * HEAD_DIM
N_PREV_CHUNKS = 8
REL_CLIP = 128
D_CONV = 256
CONV_WIDTH = 31
D_FF = 4 * D_MODEL
N_BRANCH = 3
ALPHA = (2.0 * DEPTH) ** 0.25
LN_EPS = 1e-5
NEG_INF = -1e30
LOG2_E = 1.4426950408889634

OFF_POOL = 0
OFF_Q = OFF_POOL + D_POOL
OFF_K = OFF_Q + D_ATTN
OFF_V = OFF_K + D_ATTN
OFF_CONV = OFF_V + D_ATTN
OFF_GATE = OFF_CONV + 2 * D_CONV
D_IN = OFF_GATE + N_BRANCH * D_MODEL

LANES = 128
SUBLANES = 8
MIX_TILE = 512
CHUNKS_PER_TILE = MIX_TILE // CHUNK
BAND_CHUNKS = N_PREV_CHUNKS + 2
BAND = BAND_CHUNKS * CHUNK
KV_HIST = (BAND_CHUNKS - 1) * CHUNK
CONV_HIST = 32
POOL_HIST = 16
HEAD_PAIRS = N_HEADS // 2
SCORE_SLOTS = 2 * HEAD_PAIRS
COL_BLOCK = 256
FFN_TILE = 512
FF_BLOCK = 512
ADA_BLOCK = 1536
MIX_VMEM_LIMIT = 52 * 1024 * 1024
FFN_VMEM_LIMIT = 48 * 1024 * 1024
ADA_VMEM_LIMIT = 32 * 1024 * 1024

_F32 = jnp.float32
_BF16 = jnp.bfloat16


def _dot(a, b):
    return jnp.dot(a, b, preferred_element_type=_F32)


def _normalize(x):
    mu = jnp.mean(x, axis=-1, keepdims=True)
    xc = x - mu
    var = jnp.mean(xc * xc, axis=-1, keepdims=True)
    return xc * lax.rsqrt(var + LN_EPS)


def _zero_token(v):
    rows, cols = v.shape
    folded = v.reshape(rows // SUBLANES, SUBLANES, cols).sum(axis=0)
    folded = sum(folded[:, k * LANES:(k + 1) * LANES] for k in range(cols // LANES))
    bits = lax.bitcast_convert_type(folded, jnp.uint32)
    half = jnp.uint32(16)
    zero = lax.shift_right_logical(lax.shift_right_logical(bits, half), half)
    return lax.bitcast_convert_type(zero, _F32)[0:1, :]


def _ada_kernel(c_ref, w_ref, b_ref, o_ref):
    c = c_ref[...]
    ca = (c * jax.nn.sigmoid(c)).astype(_BF16)
    o_ref[...] = _dot(ca, w_ref[...].astype(_BF16)) + b_ref[...]


def _ada_call(c, w_ada, b_ada):
    depth, d, n = w_ada.shape
    batch = c.shape[0]
    return pl.pallas_call(
        _ada_kernel,
        grid=(depth, n // ADA_BLOCK),
        in_specs=[
            pl.BlockSpec((batch, d), lambda l, j: (0, 0)),
            pl.BlockSpec((None, d, ADA_BLOCK), lambda l, j: (l, 0, j)),
            pl.BlockSpec((None, 1, ADA_BLOCK), lambda l, j: (l, 0, j)),
        ],
        out_specs=pl.BlockSpec((None, batch, ADA_BLOCK), lambda l, j: (l, 0, j)),
        out_shape=jax.ShapeDtypeStruct((depth, batch, n), _F32),
        compiler_params=pltpu.CompilerParams(
            dimension_semantics=("arbitrary", "arbitrary"), vmem_limit_bytes=ADA_VMEM_LIMIT),
        name="ada_mod",
    )(c, w_ada, b_ada.reshape(depth, 1, n))


def _mixer_kernel(x_ref, mod_ref, w_in_ref, b_gate_ref, w_pool_ref, pool_scale_ref, bias_ref,
                  conv_w_ref, conv_b_ref, conv_g_ref, conv_beta_ref, w_bp_ref, w_ba_ref, w_bc_ref,
                  w_o_ref, ln_g_ref, ln_b_ref, o_ref,
                  u_buf, qe_buf, qo_buf, k_buf, v_buf, attn_buf, pool_buf, conv_buf, conv_shift,
                  score_buf, rowmax_buf, prob_buf, gate_buf, partial_buf, merged_buf):
    T, D = MIX_TILE, D_MODEL
    i = pl.program_id(1)

    @pl.when(i == 0)
    def _():
        k_buf[0:KV_HIST, :] = jnp.zeros((KV_HIST, D_ATTN), _BF16)
        for p in range(HEAD_PAIRS):
            v_buf[0:KV_HIST, 2 * p * LANES:(2 * p + 1) * LANES] = jnp.zeros((KV_HIST, LANES), _BF16)
            v_buf[:, (2 * p + 1) * LANES:2 * (p + 1) * LANES] = jnp.ones((KV_HIST + MIX_TILE, LANES), _BF16)
        pool_buf[0:POOL_HIST, :] = jnp.zeros((POOL_HIST, D_POOL), _F32)
        conv_buf[0:CONV_HIST, :] = jnp.zeros((CONV_HIST, D_CONV), _F32)

    sh_m = mod_ref[:, 0:D]
    sc_m = mod_ref[:, D:2 * D]
    g_m = mod_ref[:, 2 * D:3 * D]
    halves = (slice(0, T // 2), slice(T // 2, T))

    def in_proj(off, width, rows=slice(0, T)):
        return _dot(u_buf[rows, :], w_in_ref[:, off:off + width])

    lane = lax.broadcasted_iota(jnp.int32, (CHUNK, LANES), 1)
    key_idx = lax.broadcasted_iota(jnp.int32, (1, BAND), 1)
    dyn0 = jnp.minimum(i, 0)

    def scores(j):
        chunk = i * CHUNKS_PER_TILE + j
        first_valid = jnp.maximum(BAND_CHUNKS - 1 - chunk, 1) * CHUNK
        mask_add = jnp.where(key_idx >= first_valid, 0.0, NEG_INF)
        rows = slice(j * CHUNK, (j + 1) * CHUNK)
        band = slice(j * CHUNK, j * CHUNK + BAND)
        for p in range(HEAD_PAIRS):
            cs = slice(p * LANES, (p + 1) * LANES)
            slot = (j % 2) * HEAD_PAIRS + p + dyn0
            qs = jnp.concatenate([qe_buf[rows, cs], qo_buf[rows, cs]], axis=0)
            sc = lax.dot_general(qs, k_buf[band, cs], (((1,), (1,)), ((), ())),
                                 preferred_element_type=_F32) + bias_ref[p] + mask_add
            score_buf[slot] = sc
            rowmax_buf[slot] = jnp.max(sc, axis=-1, keepdims=True)

    def softmax(j):
        for p in range(HEAD_PAIRS):
            slot = (j % 2) * HEAD_PAIRS + p + dyn0
            prob_buf[slot] = jnp.exp2(score_buf[slot] - rowmax_buf[slot]).astype(_BF16)

    def weighted_values(j):
        rows = slice(j * CHUNK, (j + 1) * CHUNK)
        band = slice(j * CHUNK, j * CHUNK + BAND)
        for p in range(HEAD_PAIRS):
            slot = (j % 2) * HEAD_PAIRS + p + dyn0
            pv = _dot(prob_buf[slot], v_buf[band, 2 * p * LANES:2 * (p + 1) * LANES])
            pv = pv[:, 0:LANES] * (1.0 / pv[:, LANES:2 * LANES])
            attn_buf[rows, p * LANES:(p + 1) * LANES] = jnp.where(
                lane < HEAD_DIM, pv[0:CHUNK], pv[CHUNK:2 * CHUNK]).astype(_BF16)

    def gates(first, count):
        for g in range(first, first + count):
            cs = slice(g * COL_BLOCK, (g + 1) * COL_BLOCK)
            gate_buf[:, cs] = jax.nn.sigmoid(in_proj(OFF_GATE + g * COL_BLOCK, COL_BLOCK) + b_gate_ref[:, cs])

    even_head = (lax.broadcasted_iota(jnp.int32, (T // 2, D_ATTN), 1) & (LANES - 1)) < HEAD_DIM
    for rows in halves:
        u_buf[rows, :] = (_normalize(x_ref[rows, :]) * (1.0 + sc_m) + sh_m).astype(_BF16)
        q = in_proj(OFF_Q, D_ATTN, rows) * (HEAD_DIM ** -0.5 * LOG2_E)
        qe_buf[rows, :] = jnp.where(even_head, q, 0.0).astype(_BF16)
        qo_buf[rows, :] = jnp.where(even_head, 0.0, q).astype(_BF16)
        kv_rows = slice(KV_HIST + rows.start, KV_HIST + rows.stop)
        k_buf[kv_rows, :] = in_proj(OFF_K, D_ATTN, rows).astype(_BF16)
        v = in_proj(OFF_V, D_ATTN, rows).astype(_BF16)
        for p in range(HEAD_PAIRS):
            v_buf[kv_rows, 2 * p * LANES:(2 * p + 1) * LANES] = v[:, p * LANES:(p + 1) * LANES]
    scores(0)

    pool_buf[POOL_HIST:POOL_HIST + T, :] = in_proj(OFF_POOL, D_POOL)
    zc = in_proj(OFF_CONV, 2 * D_CONV)
    conv_buf[CONV_HIST:CONV_HIST + T, :] = zc[:, 0:D_CONV] * jax.nn.sigmoid(zc[:, D_CONV:2 * D_CONV])
    softmax(0)

    t_abs = i * T + lax.broadcasted_iota(jnp.int32, (T, LANES), 0)
    upper_half = lax.broadcasted_iota(jnp.int32, (T, LANES), 1) >= POOL_GROUP
    pooled = []
    for col, (w_lo, w_hi) in enumerate(((POOL_WINDOWS[0], POOL_WINDOWS[1]), (POOL_WINDOWS[2], POOL_WINDOWS[3]))):
        cs = slice(col * LANES, (col + 1) * LANES)
        x0 = pool_buf[POOL_HIST:POOL_HIST + T, cs]
        acc = x0
        for sh in range(1, w_hi):
            xs = pool_buf[POOL_HIST - sh:POOL_HIST - sh + T, cs]
            acc = acc + (xs if sh < w_lo else jnp.where(upper_half, xs, 0.0))
        window = jnp.where(upper_half, w_hi, w_lo)
        count = jnp.minimum(t_abs + 1, window).astype(_F32)
        pooled.append(acc / count - x0)
    pooled = jnp.concatenate(pooled, axis=1).astype(_BF16)
    pool_feat = (_dot(pooled, w_pool_ref[...]) * pool_scale_ref[...]).astype(_BF16)

    gate_blocks = N_BRANCH * D // COL_BLOCK
    for j in range(CHUNKS_PER_TILE):
        if j + 1 < CHUNKS_PER_TILE:
            scores(j + 1)
        weighted_values(j)
        first = j * gate_blocks // CHUNKS_PER_TILE
        gates(first, (j + 1) * gate_blocks // CHUNKS_PER_TILE - first)
        if j + 1 < CHUNKS_PER_TILE:
            softmax(j + 1)

    span = CONV_HIST + T - SUBLANES
    for r in range(1, SUBLANES):
        conv_shift[r - 1, 0:span, :] = conv_buf[r:r + span, :]
    acc = None
    for tap in range(CONV_WIDTH):
        start = CONV_HIST - (CONV_WIDTH - 1) + tap
        base, r = (start // SUBLANES) * SUBLANES, start % SUBLANES
        rows = conv_buf[base:base + T, :] if r == 0 else conv_shift[r - 1, base:base + T, :]
        term = conv_w_ref[tap:tap + 1, :] * rows
        acc = term if acc is None else acc + term
    hc = _normalize(acc + conv_b_ref[...]) * conv_g_ref[...] + conv_beta_ref[...]
    conv_feat = (hc * jax.nn.sigmoid(hc)).astype(_BF16)

    for cch in range(KV_HIST // CHUNK):
        dst = slice(cch * CHUNK, (cch + 1) * CHUNK)
        src = slice(T + cch * CHUNK, T + (cch + 1) * CHUNK)
        k_buf[dst, :] = k_buf[src, :]
        for p in range(HEAD_PAIRS):
            vs = slice(2 * p * LANES, (2 * p + 1) * LANES)
            v_buf[dst, vs] = v_buf[src, vs]
    pool_buf[0:POOL_HIST, :] = pool_buf[T:T + POOL_HIST, :]
    conv_buf[0:CONV_HIST, :] = conv_buf[T:T + CONV_HIST, :]

    attn_feat = attn_buf[...]
    col_blocks = D // COL_BLOCK
    for nb in range(col_blocks):
        cs = slice(nb * COL_BLOCK, (nb + 1) * COL_BLOCK)
        partial_buf[:, cs] = (gate_buf[:, cs] * _dot(pool_feat, w_bp_ref[:, cs])
                              + gate_buf[:, D + nb * COL_BLOCK:D + (nb + 1) * COL_BLOCK]
                              * _dot(attn_feat, w_ba_ref[:, cs]))
    for nb in range(col_blocks):
        cs = slice(nb * COL_BLOCK, (nb + 1) * COL_BLOCK)
        merged = (partial_buf[:, cs] + gate_buf[:, 2 * D + nb * COL_BLOCK:2 * D + (nb + 1) * COL_BLOCK]
                  * _dot(conv_feat, w_bc_ref[:, cs]))
        merged_buf[:, cs] = merged.astype(_BF16)

    token = None
    for rows in halves:
        gain = g_m if token is None else g_m + jnp.concatenate([token] * (D // LANES), axis=1)
        r = ALPHA * x_ref[rows, :] + gain * _dot(merged_buf[rows, :], w_o_ref[...])
        y = _normalize(r) * ln_g_ref[...] + ln_b_ref[...]
        o_ref[rows, :] = y
        token = _zero_token(y)


def _layer_spec(stacked, layer):
    index = (layer,) + (0,) * (stacked.ndim - 1)
    return pl.BlockSpec((None,) + stacked.shape[1:], lambda *_: index, pipeline_mode=pl.Buffered(1))


def _mixer_call(layer, x, mod, w_in, b_gate, w_pool_bd, pool_scale, bias_pairs, conv_w, conv_b,
                conv_g, conv_beta, w_bp, w_ba, w_bc, w_o, ln_g, ln_b):
    B, S, D = x.shape
    T = MIX_TILE
    consts = (w_in, b_gate, w_pool_bd, pool_scale, bias_pairs, conv_w, conv_b, conv_g, conv_beta,
              w_bp, w_ba, w_bc, w_o, ln_g, ln_b)
    return pl.pallas_call(
        _mixer_kernel,
        grid=(B, S // T),
        in_specs=[pl.BlockSpec((None, T, D), lambda b, i: (b, i, 0)),
                  pl.BlockSpec((None, None, 1, mod.shape[-1]), lambda b, i: (layer, b, 0, 0))]
                 + [_layer_spec(a, layer) for a in consts],
        out_specs=pl.BlockSpec((None, T, D), lambda b, i: (b, i, 0)),
        out_shape=jax.ShapeDtypeStruct((B, S, D), _F32),
        scratch_shapes=[
            pltpu.VMEM((T, D), _BF16),
            pltpu.VMEM((T, D_ATTN), _BF16),
            pltpu.VMEM((T, D_ATTN), _BF16),
            pltpu.VMEM((KV_HIST + T, D_ATTN), _BF16),
            pltpu.VMEM((KV_HIST + T, 2 * D_ATTN), _BF16),
            pltpu.VMEM((T, D_ATTN), _BF16),
            pltpu.VMEM((POOL_HIST + T, D_POOL), _F32),
            pltpu.VMEM((CONV_HIST + T, D_CONV), _F32),
            pltpu.VMEM((SUBLANES - 1, CONV_HIST + T, D_CONV), _F32),
            pltpu.VMEM((SCORE_SLOTS, 2 * CHUNK, BAND), _F32),
            pltpu.VMEM((SCORE_SLOTS, 2 * CHUNK, 1), _F32),
            pltpu.VMEM((SCORE_SLOTS, 2 * CHUNK, BAND), _BF16),
            pltpu.VMEM((T, N_BRANCH * D), _F32),
            pltpu.VMEM((T, D), _F32),
            pltpu.VMEM((T, D), _BF16),
        ],
        compiler_params=pltpu.CompilerParams(
            dimension_semantics=("arbitrary", "arbitrary"), vmem_limit_bytes=MIX_VMEM_LIMIT),
        name="token_mixer",
    )(x, mod, *consts)


def _ffn_kernel(x_ref, mod_ref, w1_ref, b1_ref, w2_ref, b2_ref, ln_g_ref, ln_b_ref, o_ref,
                u_even, u_odd, x_even, x_odd, r_even, r_odd, h_buf, *, n_tiles, tiles_per_seq):
    D = D_MODEL
    s = pl.program_id(0)
    n_blocks = D_FF // FF_BLOCK
    rows_per_slice = FFN_TILE // n_blocks

    def batch_of(tile):
        return lax.div(jnp.clip(tile, 0, n_tiles - 1), tiles_per_seq)

    def pre(rs, u_new, x_new):
        mod_new = mod_ref[batch_of(s)]
        x = x_ref[rs, :]
        x_new[rs, :] = x
        u = _normalize(x) * (1.0 + mod_new[:, 4 * D:5 * D]) + mod_new[:, 3 * D:4 * D]
        u_new[rs, :] = u.astype(_BF16)
        return u

    def post(rs, r_old):
        y = _normalize(r_old[rs, :]) * ln_g_ref[...] + ln_b_ref[...]
        o_ref[rs, :] = y
        return y

    def step(u_new, u_mid, x_new, x_mid, r_old, r_mid):
        mod_mid = mod_ref[batch_of(s - 1)]
        for c in range(n_blocks):
            rs = slice(c * rows_per_slice, (c + 1) * rows_per_slice)
            token = _zero_token(post(rs, r_old)) + _zero_token(pre(rs, u_new, x_new))
            cs = slice(c * FF_BLOCK, (c + 1) * FF_BLOCK)
            bias = b1_ref[:, cs] + jnp.concatenate([token] * (FF_BLOCK // LANES), axis=1)
            h = jnp.maximum(_dot(u_mid[...], w1_ref[:, cs]) + bias, 0.0)
            h_buf[:, cs] = (h * h).astype(_BF16)
        g_f = mod_mid[:, 5 * D:6 * D]
        for n in range(D // COL_BLOCK):
            ns = slice(n * COL_BLOCK, (n + 1) * COL_BLOCK)
            ff = _dot(h_buf[...], w2_ref[:, ns]) + b2_ref[:, ns]
            r_mid[:, ns] = ALPHA * x_mid[:, ns] + g_f[:, ns] * ff

    all_rows = slice(0, FFN_TILE)
    in_steady_state = jnp.logical_and(s >= 1, s <= n_tiles)
    parity = lax.rem(s, 2)
    r_last = r_odd if (n_tiles - 1) % 2 else r_even

    @pl.when(s == 0)
    def _():
        r_odd[...] = jnp.zeros(r_odd.shape, _F32)
        pre(all_rows, u_even, x_even)

    @pl.when(jnp.logical_and(in_steady_state, parity == 0))
    def _():
        step(u_even, u_odd, x_even, x_odd, r_even, r_odd)

    @pl.when(jnp.logical_and(in_steady_state, parity == 1))
    def _():
        step(u_odd, u_even, x_odd, x_even, r_odd, r_even)

    @pl.when(s == n_tiles + 1)
    def _():
        post(all_rows, r_last)


def _tile_spec(tile, n_tiles, lag):
    return pl.BlockSpec((tile, D_MODEL), lambda s: (jnp.clip(s - lag, 0, n_tiles - 1), 0))


def _ffn_call(layer, x, mod, w1, b1, w2, b2, ln_g, ln_b):
    B, S, D = x.shape
    T = FFN_TILE
    n_tiles = B * S // T
    consts = (w1, b1, w2, b2, ln_g, ln_b)
    kernel_fn = functools.partial(_ffn_kernel, n_tiles=n_tiles, tiles_per_seq=S // T)
    out = pl.pallas_call(
        kernel_fn,
        grid=(n_tiles + 2,),
        in_specs=[_tile_spec(T, n_tiles, 0),
                  pl.BlockSpec((None,) + mod.shape[1:], lambda s: (layer, 0, 0, 0))]
                 + [_layer_spec(a, layer) for a in consts],
        out_specs=_tile_spec(T, n_tiles, 2),
        out_shape=jax.ShapeDtypeStruct((B * S, D), _F32),
        scratch_shapes=[pltpu.VMEM((T, D), _BF16), pltpu.VMEM((T, D), _BF16),
                        pltpu.VMEM((T, D), _F32), pltpu.VMEM((T, D), _F32),
                        pltpu.VMEM((T, D), _F32), pltpu.VMEM((T, D), _F32),
                        pltpu.VMEM((T, D_FF), _BF16)],
        compiler_params=pltpu.CompilerParams(
            dimension_semantics=("arbitrary",), vmem_limit_bytes=FFN_VMEM_LIMIT),
        name="relu2_mlp",
    )(x.reshape(B * S, D), mod, *consts)
    return out.reshape(B, S, D)


def _bias_pairs(rel_bias):
    period = BAND + CHUNK - 1
    m = np.arange(period)
    ext_idx = np.clip(BAND - 1 - m, -REL_CLIP, REL_CLIP) + REL_CLIP
    ext = rel_bias[:, :, ext_idx] * LOG2_E
    lead = ext.shape[:-1]
    hankel = jnp.tile(ext, (1, 1, CHUNK + 1))[..., :CHUNK * (period + 1)]
    hankel = hankel.reshape(*lead, CHUNK, period + 1)[..., :BAND]
    bias = hankel[..., ::-1, :]
    return bias.reshape(lead[0], HEAD_PAIRS, 2 * CHUNK, BAND)


def _block_diag(w):
    depth, groups, gin, gout = w.shape
    eye = jnp.eye(groups, dtype=w.dtype)[None, :, None, :, None]
    return (w[:, :, :, None, :] * eye).reshape(depth, groups * gin, groups * gout)


def kernel(x, c, w_ada, b_ada, w_in, b_gate, w_pool, pool_scale, rel_bias, conv_w, conv_b, conv_ln_g, conv_ln_b, w_br_pool, w_br_attn, w_br_conv, w_o, ln_mix_g, ln_mix_b, w_ff1, b_ff1, w_ff2, b_ff2, ln_ff_g, ln_ff_b):
    depth = w_in.shape[0]
    bf16 = lambda a: a.astype(_BF16)
    rows = lambda a: a.reshape(depth, 1, -1)
    mod = _ada_call(c, w_ada, b_ada)[:, :, None, :]
    mixer_params = (bf16(w_in), rows(b_gate), bf16(_block_diag(w_pool)), rows(pool_scale),
                    _bias_pairs(rel_bias), conv_w, rows(conv_b), rows(conv_ln_g), rows(conv_ln_b),
                    bf16(w_br_pool), bf16(w_br_attn), bf16(w_br_conv), bf16(w_o),
                    rows(ln_mix_g), rows(ln_mix_b))
    ffn_params = (bf16(w_ff1), rows(b_ff1), bf16(w_ff2), rows(b_ff2), rows(ln_ff_g), rows(ln_ff_b))
    for layer in range(depth):
        x = _mixer_call(layer, x, mod, *mixer_params)
        x = _ffn_call(layer, x, mod, *ffn_params)
    return x
```

```python
import functools

import jax
import jax.numpy as jnp
import numpy as np
from jax import lax
from jax.experimental import pallas as pl
from jax.experimental.pallas import tpu as pltpu

D_MODEL = 1024
DEPTH = 2
CHUNK = 64
POOL_WINDOWS = (2, 4, 8, 16)
POOL_GROUP = 64
D_POOL = POOL_GROUP * len(POOL_WINDOWS)
N_HEADS = 8
HEAD_DIM = 64
D_ATTN = N_HEADS * HEAD_DIM
N_PREV_CHUNKS = 8
REL_CLIP = 128
D_CONV = 256
CONV_WIDTH = 31
D_FF = 4 * D_MODEL
N_BRANCH = 3
ALPHA = (2.0 * DEPTH) ** 0.25
LN_EPS = 1e-5
NEG_INF = -1e30
LOG2_E = 1.4426950408889634

OFF_POOL = 0
OFF_Q = OFF_POOL + D_POOL
OFF_K = OFF_Q + D_ATTN
OFF_V = OFF_K + D_ATTN
OFF_CONV = OFF_V + D_ATTN
OFF_GATE = OFF_CONV + 2 * D_CONV
D_IN = OFF_GATE + N_BRANCH * D_MODEL

LANES = 128
SUBLANES = 8
MIX_TILE = 512
CHUNKS_PER_TILE = MIX_TILE // CHUNK
BAND_CHUNKS = N_PREV_CHUNKS + 2
BAND = BAND_CHUNKS * CHUNK
KV_HIST = (BAND_CHUNKS - 1) * CHUNK
CONV_HIST = 32
POOL_HIST = 16
HEAD_PAIRS = N_HEADS // 2
SCORE_SLOTS = 2 * HEAD_PAIRS
COL_BLOCK = 512
FFN_TILE = 512
FF_BLOCK = 512
ADA_BLOCK = 1536
MIX_VMEM_LIMIT = 52 * 1024 * 1024
FFN_VMEM_LIMIT = 48 * 1024 * 1024
ADA_VMEM_LIMIT = 32 * 1024 * 1024

_F32 = jnp.float32
_BF16 = jnp.bfloat16


def _dot(a, b):
    return jnp.dot(a, b, preferred_element_type=_F32)


def _normalize(x):
    mu = jnp.mean(x, axis=-1, keepdims=True)
    xc = x - mu
    var = jnp.mean(xc * xc, axis=-1, keepdims=True)
    return xc * lax.rsqrt(var + LN_EPS)


def _zero_token(v):
    rows, cols = v.shape
    folded = v.reshape(rows // SUBLANES, SUBLANES, cols).sum(axis=0)
    folded = sum(folded[:, k * LANES:(k + 1) * LANES] for k in range(cols // LANES))
    bits = lax.bitcast_convert_type(folded, jnp.uint32)
    half = jnp.uint32(16)
    zero = lax.shift_right_logical(lax.shift_right_logical(bits, half), half)
    return lax.bitcast_convert_type(zero, _F32)[0:1, :]


def _ada_kernel(c_ref, w_ref, b_ref, o_ref):
    c = c_ref[...]
    ca = (c * jax.nn.sigmoid(c)).astype(_BF16)
    o_ref[...] = _dot(ca, w_ref[...].astype(_BF16)) + b_ref[...]


def _ada_call(c, w_ada, b_ada):
    depth, d, n = w_ada.shape
    batch = c.shape[0]
    return pl.pallas_call(
        _ada_kernel,
        grid=(depth, n // ADA_BLOCK),
        in_specs=[
            pl.BlockSpec((batch, d), lambda l, j: (0, 0)),
            pl.BlockSpec((None, d, ADA_BLOCK), lambda l, j: (l, 0, j)),
            pl.BlockSpec((None, 1, ADA_BLOCK), lambda l, j: (l, 0, j)),
        ],
        out_specs=pl.BlockSpec((None, batch, ADA_BLOCK), lambda l, j: (l, 0, j)),
        out_shape=jax.ShapeDtypeStruct((depth, batch, n), _F32),
        compiler_params=pltpu.CompilerParams(
            dimension_semantics=("arbitrary", "arbitrary"), vmem_limit_bytes=ADA_VMEM_LIMIT),
        name="ada_mod",
    )(c, w_ada, b_ada.reshape(depth, 1, n))


def _mixer_kernel(x_ref, mod_ref, w_in_ref, b_gate_ref, w_pool_ref, pool_scale_ref, bias_ref,
                  conv_w_ref, conv_b_ref, conv_g_ref, conv_beta_ref, w_bp_ref, w_ba_ref, w_bc_ref,
                  w_o_ref, ln_g_ref, ln_b_ref, o_ref,
                  u_buf, qe_buf, qo_buf, k_buf, v_buf, attn_buf, pool_buf, conv_buf, conv_shift,
                  score_buf, rowmax_buf, prob_buf, gate_buf, partial_buf, merged_buf):
    T, D = MIX_TILE, D_MODEL
    i = pl.program_id(1)

    @pl.when(i == 0)
    def _():
        k_buf[0:KV_HIST, :] = jnp.zeros((KV_HIST, D_ATTN), _BF16)
        for p in range(HEAD_PAIRS):
            v_buf[0:KV_HIST, 2 * p * LANES:(2 * p + 1) * LANES] = jnp.zeros((KV_HIST, LANES), _BF16)
            v_buf[:, (2 * p + 1) * LANES:2 * (p + 1) * LANES] = jnp.ones((KV_HIST + MIX_TILE, LANES), _BF16)
        pool_buf[0:POOL_HIST, :] = jnp.zeros((POOL_HIST, D_POOL), _F32)
        conv_buf[0:CONV_HIST, :] = jnp.zeros((CONV_HIST, D_CONV), _F32)

    sh_m = mod_ref[:, 0:D]
    sc_m = mod_ref[:, D:2 * D]
    g_m = mod_ref[:, 2 * D:3 * D]
    halves = (slice(0, T // 2), slice(T // 2, T))

    def in_proj(off, width, rows=slice(0, T)):
        return _dot(u_buf[rows, :], w_in_ref[:, off:off + width])

    lane = lax.broadcasted_iota(jnp.int32, (CHUNK, LANES), 1)
    key_idx = lax.broadcasted_iota(jnp.int32, (1, BAND), 1)
    dyn0 = jnp.minimum(i, 0)

    def scores(j):
        chunk = i * CHUNKS_PER_TILE + j
        first_valid = jnp.maximum(BAND_CHUNKS - 1 - chunk, 1) * CHUNK
        mask_add = jnp.where(key_idx >= first_valid, 0.0, NEG_INF)
        rows = slice(j * CHUNK, (j + 1) * CHUNK)
        band = slice(j * CHUNK, j * CHUNK + BAND)
        for p in range(HEAD_PAIRS):
            cs = slice(p * LANES, (p + 1) * LANES)
            slot = (j % 2) * HEAD_PAIRS + p + dyn0
            qs = jnp.concatenate([qe_buf[rows, cs], qo_buf[rows, cs]], axis=0)
            sc = lax.dot_general(qs, k_buf[band, cs], (((1,), (1,)), ((), ())),
                                 preferred_element_type=_F32) + bias_ref[p] + mask_add
            score_buf[slot] = sc
            rowmax_buf[slot] = jnp.max(sc, axis=-1, keepdims=True)

    def softmax(j):
        for p in range(HEAD_PAIRS):
            slot = (j % 2) * HEAD_PAIRS + p + dyn0
            prob_buf[slot] = jnp.exp2(score_buf[slot] - rowmax_buf[slot]).astype(_BF16)

    def weighted_values(j):
        rows = slice(j * CHUNK, (j + 1) * CHUNK)
        band = slice(j * CHUNK, j * CHUNK + BAND)
        for p in range(HEAD_PAIRS):
            slot = (j % 2) * HEAD_PAIRS + p + dyn0
            pv = _dot(prob_buf[slot], v_buf[band, 2 * p * LANES:2 * (p + 1) * LANES])
            pv = pv[:, 0:LANES] * (1.0 / pv[:, LANES:2 * LANES])
            attn_buf[rows, p * LANES:(p + 1) * LANES] = jnp.where(
                lane < HEAD_DIM, pv[0:CHUNK], pv[CHUNK:2 * CHUNK]).astype(_BF16)

    def gates(first, count):
        for g in range(first, first + count):
            cs = slice(g * COL_BLOCK, (g + 1) * COL_BLOCK)
            gate_buf[:, cs] = jax.nn.sigmoid(in_proj(OFF_GATE + g * COL_BLOCK, COL_BLOCK) + b_gate_ref[:, cs])

    even_head = (lax.broadcasted_iota(jnp.int32, (T // 2, D_ATTN), 1) & (LANES - 1)) < HEAD_DIM
    for rows in halves:
        u_buf[rows, :] = (_normalize(x_ref[rows, :]) * (1.0 + sc_m) + sh_m).astype(_BF16)
        q = in_proj(OFF_Q, D_ATTN, rows) * (HEAD_DIM ** -0.5 * LOG2_E)
        qe_buf[rows, :] = jnp.where(even_head, q, 0.0).astype(_BF16)
        qo_buf[rows, :] = jnp.where(even_head, 0.0, q).astype(_BF16)
        kv_rows = slice(KV_HIST + rows.start, KV_HIST + rows.stop)
        k_buf[kv_rows, :] = in_proj(OFF_K, D_ATTN, rows).astype(_BF16)
        v = in_proj(OFF_V, D_ATTN, rows).astype(_BF16)
        for p in range(HEAD_PAIRS):
            v_buf[kv_rows, 2 * p * LANES:(2 * p + 1) * LANES] = v[:, p * LANES:(p + 1) * LANES]
    scores(0)

    pool_buf[POOL_HIST:POOL_HIST + T, :] = in_proj(OFF_POOL, D_POOL)
    zc = in_proj(OFF_CONV, 2 * D_CONV)
    conv_buf[CONV_HIST:CONV_HIST + T, :] = zc[:, 0:D_CONV] * jax.nn.sigmoid(zc[:, D_CONV:2 * D_CONV])
    softmax(0)

    t_abs = i * T + lax.broadcasted_iota(jnp.int32, (T, LANES), 0)
    upper_half = lax.broadcasted_iota(jnp.int32, (T, LANES), 1) >= POOL_GROUP
    pooled = []
    for col, (w_lo, w_hi) in enumerate(((POOL_WINDOWS[0], POOL_WINDOWS[1]), (POOL_WINDOWS[2], POOL_WINDOWS[3]))):
        cs = slice(col * LANES, (col + 1) * LANES)
        x0 = pool_buf[POOL_HIST:POOL_HIST + T, cs]
        acc = x0
        for sh in range(1, w_hi):
            xs = pool_buf[POOL_HIST - sh:POOL_HIST - sh + T, cs]
            acc = acc + (xs if sh < w_lo else jnp.where(upper_half, xs, 0.0))
        window = jnp.where(upper_half, w_hi, w_lo)
        count = jnp.minimum(t_abs + 1, window).astype(_F32)
        pooled.append(acc / count - x0)
    pooled = jnp.concatenate(pooled, axis=1).astype(_BF16)
    pool_feat = (_dot(pooled, w_pool_ref[...]) * pool_scale_ref[...]).astype(_BF16)

    gate_blocks = N_BRANCH * D // COL_BLOCK
    for j in range(CHUNKS_PER_TILE):
        if j + 1 < CHUNKS_PER_TILE:
            scores(j + 1)
        weighted_values(j)
        first = j * gate_blocks // CHUNKS_PER_TILE
        gates(first, (j + 1) * gate_blocks // CHUNKS_PER_TILE - first)
        if j + 1 < CHUNKS_PER_TILE:
            softmax(j + 1)

    span = CONV_HIST + T - SUBLANES
    for r in range(1, SUBLANES):
        conv_shift[r - 1, 0:span, :] = conv_buf[r:r + span, :]
    acc = None
    for tap in range(CONV_WIDTH):
        start = CONV_HIST - (CONV_WIDTH - 1) + tap
        base, r = (start // SUBLANES) * SUBLANES, start % SUBLANES
        rows = conv_buf[base:base + T, :] if r == 0 else conv_shift[r - 1, base:base + T, :]
        term = conv_w_ref[tap:tap + 1, :] * rows
        acc = term if acc is None else acc + term
    hc = _normalize(acc + conv_b_ref[...]) * conv_g_ref[...] + conv_beta_ref[...]
    conv_feat = (hc * jax.nn.sigmoid(hc)).astype(_BF16)

    for cch in range(KV_HIST // CHUNK):
        dst = slice(cch * CHUNK, (cch + 1) * CHUNK)
        src = slice(T + cch * CHUNK, T + (cch + 1) * CHUNK)
        k_buf[dst, :] = k_buf[src, :]
        for p in range(HEAD_PAIRS):
            vs = slice(2 * p * LANES, (2 * p + 1) * LANES)
            v_buf[dst, vs] = v_buf[src, vs]
    pool_buf[0:POOL_HIST, :] = pool_buf[T:T + POOL_HIST, :]
    conv_buf[0:CONV_HIST, :] = conv_buf[T:T + CONV_HIST, :]

    attn_feat = attn_buf[...]
    col_blocks = D // COL_BLOCK
    for nb in range(col_blocks):
        cs = slice(nb * COL_BLOCK, (nb + 1) * COL_BLOCK)
        partial_buf[:, cs] = (gate_buf[:, cs] * _dot(pool_feat, w_bp_ref[:, cs])
                              + gate_buf[:, D + nb * COL_BLOCK:D + (nb + 1) * COL_BLOCK]
                              * _dot(attn_feat, w_ba_ref[:, cs]))
    for nb in range(col_blocks):
        cs = slice(nb * COL_BLOCK, (nb + 1) * COL_BLOCK)
        merged = (partial_buf[:, cs] + gate_buf[:, 2 * D + nb * COL_BLOCK:2 * D + (nb + 1) * COL_BLOCK]
                  * _dot(conv_feat, w_bc_ref[:, cs]))
        merged_buf[:, cs] = merged.astype(_BF16)

    token = None
    for rows in halves:
        gain = g_m if token is None else g_m + jnp.concatenate([token] * (D // LANES), axis=1)
        r = ALPHA * x_ref[rows, :] + gain * _dot(merged_buf[rows, :], w_o_ref[...])
        y = _normalize(r) * ln_g_ref[...] + ln_b_ref[...]
        o_ref[rows, :] = y
        token = _zero_token(y)


def _layer_spec(stacked, layer):
    index = (layer,) + (0,) * (stacked.ndim - 1)
    return pl.BlockSpec((None,) + stacked.shape[1:], lambda *_: index, pipeline_mode=pl.Buffered(1))


def _mixer_call(layer, x, mod, w_in, b_gate, w_pool_bd, pool_scale, bias_pairs, conv_w, conv_b,
                conv_g, conv_beta, w_bp, w_ba, w_bc, w_o, ln_g, ln_b):
    B, S, D = x.shape
    T = MIX_TILE
    consts = (w_in, b_gate, w_pool_bd, pool_scale, bias_pairs, conv_w, conv_b, conv_g, conv_beta,
              w_bp, w_ba, w_bc, w_o, ln_g, ln_b)
    return pl.pallas_call(
        _mixer_kernel,
        grid=(B, S // T),
        in_specs=[pl.BlockSpec((None, T, D), lambda b, i: (b, i, 0)),
                  pl.BlockSpec((None, None, 1, mod.shape[-1]), lambda b, i: (layer, b, 0, 0))]
                 + [_layer_spec(a, layer) for a in consts],
        out_specs=pl.BlockSpec((None, T, D), lambda b, i: (b, i, 0)),
        out_shape=jax.ShapeDtypeStruct((B, S, D), _F32),
        scratch_shapes=[
            pltpu.VMEM((T, D), _BF16),
            pltpu.VMEM((T, D_ATTN), _BF16),
            pltpu.VMEM((T, D_ATTN), _BF16),
            pltpu.VMEM((KV_HIST + T, D_ATTN), _BF16),
            pltpu.VMEM((KV_HIST + T, 2 * D_ATTN), _BF16),
            pltpu.VMEM((T, D_ATTN), _BF16),
            pltpu.VMEM((POOL_HIST + T, D_POOL), _F32),
            pltpu.VMEM((CONV_HIST + T, D_CONV), _F32),
            pltpu.VMEM((SUBLANES - 1, CONV_HIST + T, D_CONV), _F32),
            pltpu.VMEM((SCORE_SLOTS, 2 * CHUNK, BAND), _F32),
            pltpu.VMEM((SCORE_SLOTS, 2 * CHUNK, 1), _F32),
            pltpu.VMEM((SCORE_SLOTS, 2 * CHUNK, BAND), _BF16),
            pltpu.VMEM((T, N_BRANCH * D), _F32),
            pltpu.VMEM((T, D), _F32),
            pltpu.VMEM((T, D), _BF16),
        ],
        compiler_params=pltpu.CompilerParams(
            dimension_semantics=("arbitrary", "arbitrary"), vmem_limit_bytes=MIX_VMEM_LIMIT),
        name="token_mixer",
    )(x, mod, *consts)


def _ffn_kernel(x_ref, mod_ref, w1_ref, b1_ref, w2_ref, b2_ref, ln_g_ref, ln_b_ref, o_ref,
                u_even, u_odd, x_even, x_odd, r_even, r_odd, h_buf, *, n_tiles, tiles_per_seq):
    D = D_MODEL
    s = pl.program_id(0)
    n_blocks = D_FF // FF_BLOCK
    rows_per_slice = FFN_TILE // n_blocks

    def batch_of(tile):
        return lax.div(jnp.clip(tile, 0, n_tiles - 1), tiles_per_seq)

    def pre(rs, u_new, x_new):
        mod_new = mod_ref[batch_of(s)]
        x = x_ref[rs, :]
        x_new[rs, :] = x
        u = _normalize(x) * (1.0 + mod_new[:, 4 * D:5 * D]) + mod_new[:, 3 * D:4 * D]
        u_new[rs, :] = u.astype(_BF16)
        return u

    def post(rs, r_old):
        y = _normalize(r_old[rs, :]) * ln_g_ref[...] + ln_b_ref[...]
        o_ref[rs, :] = y
        return y

    def step(u_new, u_mid, x_new, x_mid, r_old, r_mid):
        mod_mid = mod_ref[batch_of(s - 1)]
        for c in range(n_blocks):
            rs = slice(c * rows_per_slice, (c + 1) * rows_per_slice)
            token = _zero_token(post(rs, r_old)) + _zero_token(pre(rs, u_new, x_new))
            cs = slice(c * FF_BLOCK, (c + 1) * FF_BLOCK)
            bias = b1_ref[:, cs] + jnp.concatenate([token] * (FF_BLOCK // LANES), axis=1)
            h = jnp.maximum(_dot(u_mid[...], w1_ref[:, cs]) + bias, 0.0)
            h_buf[:, cs] = (h * h).astype(_BF16)
        g_f = mod_mid[:, 5 * D:6 * D]
        for n in range(D // COL_BLOCK):
            ns = slice(n * COL_BLOCK, (n + 1) * COL_BLOCK)
            ff = _dot(h_buf[...], w2_ref[:, ns]) + b2_ref[:, ns]
            r_mid[:, ns] = ALPHA * x_mid[:, ns] + g_f[:, ns] * ff

    all_rows = slice(0, FFN_TILE)
    in_steady_state = jnp.logical_and(s >= 1, s <= n_tiles)
    parity = lax.rem(s, 2)
    r_last = r_odd if (n_tiles - 1) % 2 else r_even

    @pl.when(s == 0)
    def _():
        r_odd[...] = jnp.zeros(r_odd.shape, _F32)
        pre(all_rows, u_even, x_even)

    @pl.when(jnp.logical_and(in_steady_state, parity == 0))
    def _():
        step(u_even, u_odd, x_even, x_odd, r_even, r_odd)

    @pl.when(jnp.logical_and(in_steady_state, parity == 1))
    def _():
        step(u_odd, u_even, x_odd, x_even, r_odd, r_even)

    @pl.when(s == n_tiles + 1)
    def _():
        post(all_rows, r_last)


def _tile_spec(tile, n_tiles, lag):
    return pl.BlockSpec((tile, D_MODEL), lambda s: (jnp.clip(s - lag, 0, n_tiles - 1), 0))


def _ffn_call(layer, x, mod, w1, b1, w2, b2, ln_g, ln_b):
    B, S, D = x.shape
    T = FFN_TILE
    n_tiles = B * S // T
    consts = (w1, b1, w2, b2, ln_g, ln_b)
    kernel_fn = functools.partial(_ffn_kernel, n_tiles=n_tiles, tiles_per_seq=S // T)
    out = pl.pallas_call(
        kernel_fn,
        grid=(n_tiles + 2,),
        in_specs=[_tile_spec(T, n_tiles, 0),
                  pl.BlockSpec((None,) + mod.shape[1:], lambda s: (layer, 0, 0, 0))]
                 + [_layer_spec(a, layer) for a in consts],
        out_specs=_tile_spec(T, n_tiles, 2),
        out_shape=jax.ShapeDtypeStruct((B * S, D), _F32),
        scratch_shapes=[pltpu.VMEM((T, D), _BF16), pltpu.VMEM((T, D), _BF16),
                        pltpu.VMEM((T, D), _F32), pltpu.VMEM((T, D), _F32),
                        pltpu.VMEM((T, D), _F32), pltpu.VMEM((T, D), _F32),
                        pltpu.VMEM((T, D_FF), _BF16)],
        compiler_params=pltpu.CompilerParams(
            dimension_semantics=("arbitrary",), vmem_limit_bytes=FFN_VMEM_LIMIT),
        name="relu2_mlp",
    )(x.reshape(B * S, D), mod, *consts)
    return out.reshape(B, S, D)


def _bias_pairs(rel_bias):
    period = BAND + CHUNK - 1
    m = np.arange(period)
    ext_idx = np.clip(BAND - 1 - m, -REL_CLIP, REL_CLIP) + REL_CLIP
    ext = rel_bias[:, :, ext_idx] * LOG2_E
    lead = ext.shape[:-1]
    hankel = jnp.tile(ext, (1, 1, CHUNK + 1))[..., :CHUNK * (period + 1)]
    hankel = hankel.reshape(*lead, CHUNK, period + 1)[..., :BAND]
    bias = hankel[..., ::-1, :]
    return bias.reshape(lead[0], HEAD_PAIRS, 2 * CHUNK, BAND)


def _block_diag(w):
    depth, groups, gin, gout = w.shape
    eye = jnp.eye(groups, dtype=w.dtype)[None, :, None, :, None]
    return (w[:, :, :, None, :] * eye).reshape(depth, groups * gin, groups * gout)


def kernel(x, c, w_ada, b_ada, w_in, b_gate, w_pool, pool_scale, rel_bias, conv_w, conv_b, conv_ln_g, conv_ln_b, w_br_pool, w_br_attn, w_br_conv, w_o, ln_mix_g, ln_mix_b, w_ff1, b_ff1, w_ff2, b_ff2, ln_ff_g, ln_ff_b):
    depth = w_in.shape[0]
    bf16 = lambda a: a.astype(_BF16)
    rows = lambda a: a.reshape(depth, 1, -1)
    mod = _ada_call(c, w_ada, b_ada)[:, :, None, :]
    mixer_params = (bf16(w_in), rows(b_gate), bf16(_block_diag(w_pool)), rows(pool_scale),
                    _bias_pairs(rel_bias), conv_w, rows(conv_b), rows(conv_ln_g), rows(conv_ln_b),
                    bf16(w_br_pool), bf16(w_br_attn), bf16(w_br_conv), bf16(w_o),
                    rows(ln_mix_g), rows(ln_mix_b))
    ffn_params = (bf16(w_ff1), rows(b_ff1), bf16(w_ff2), rows(b_ff2), rows(ln_ff_g), rows(ln_ff_b))
    for layer in range(depth):
        x = _mixer_call(layer, x, mod, *mixer_params)
        x = _ffn_call(layer, x, mod, *ffn_params)
    return x
```

```python
import functools

import jax
import jax.numpy as jnp
import numpy as np
from jax import lax
from jax.experimental import pallas as pl
from jax.experimental.pallas import tpu as pltpu

D_MODEL = 1024
DEPTH = 2
CHUNK = 64
POOL_WINDOWS = (2, 4, 8, 16)
POOL_GROUP = 64
D_POOL = POOL_GROUP * len(POOL_WINDOWS)
N_HEADS = 8
HEAD_DIM = 64
D_ATTN = N_HEADS * HEAD_DIM
N_PREV_CHUNKS = 8
REL_CLIP = 128
D_CONV = 256
CONV_WIDTH = 31
D_FF = 4 * D_MODEL
N_BRANCH = 3
ALPHA = (2.0 * DEPTH) ** 0.25
LN_EPS = 1e-5
NEG_INF = -1e30
LOG2_E = 1.4426950408889634

OFF_POOL = 0
OFF_Q = OFF_POOL + D_POOL
OFF_K = OFF_Q + D_ATTN
OFF_V = OFF_K + D_ATTN
OFF_CONV = OFF_V + D_ATTN
OFF_GATE = OFF_CONV + 2 * D_CONV
D_IN = OFF_GATE + N_BRANCH * D_MODEL

LANES = 128
SUBLANES = 8
MIX_TILE = 512
CHUNKS_PER_TILE = MIX_TILE // CHUNK
BAND_CHUNKS = N_PREV_CHUNKS + 2
BAND = BAND_CHUNKS * CHUNK
KV_HIST = (BAND_CHUNKS - 1) * CHUNK
CONV_HIST = 32
POOL_HIST = 16
HEAD_PAIRS = N_HEADS // 2
SCORE_SLOTS = 2 * HEAD_PAIRS
COL_BLOCK = 256
FFN_TILE = 512
FF_BLOCK = 1024
ADA_BLOCK = 1536
MIX_VMEM_LIMIT = 52 * 1024 * 1024
FFN_VMEM_LIMIT = 48 * 1024 * 1024
ADA_VMEM_LIMIT = 32 * 1024 * 1024

_F32 = jnp.float32
_BF16 = jnp.bfloat16


def _dot(a, b):
    return jnp.dot(a, b, preferred_element_type=_F32)


def _normalize(x):
    mu = jnp.mean(x, axis=-1, keepdims=True)
    xc = x - mu
    var = jnp.mean(xc * xc, axis=-1, keepdims=True)
    return xc * lax.rsqrt(var + LN_EPS)


def _zero_token(v):
    rows, cols = v.shape
    folded = v.reshape(rows // SUBLANES, SUBLANES, cols).sum(axis=0)
    folded = sum(folded[:, k * LANES:(k + 1) * LANES] for k in range(cols // LANES))
    bits = lax.bitcast_convert_type(folded, jnp.uint32)
    half = jnp.uint32(16)
    zero = lax.shift_right_logical(lax.shift_right_logical(bits, half), half)
    return lax.bitcast_convert_type(zero, _F32)[0:1, :]


def _ada_kernel(c_ref, w_ref, b_ref, o_ref):
    c = c_ref[...]
    ca = (c * jax.nn.sigmoid(c)).astype(_BF16)
    o_ref[...] = _dot(ca, w_ref[...].astype(_BF16)) + b_ref[...]


def _ada_call(c, w_ada, b_ada):
    depth, d, n = w_ada.shape
    batch = c.shape[0]
    return pl.pallas_call(
        _ada_kernel,
        grid=(depth, n // ADA_BLOCK),
        in_specs=[
            pl.BlockSpec((batch, d), lambda l, j: (0, 0)),
            pl.BlockSpec((None, d, ADA_BLOCK), lambda l, j: (l, 0, j)),
            pl.BlockSpec((None, 1, ADA_BLOCK), lambda l, j: (l, 0, j)),
        ],
        out_specs=pl.BlockSpec((None, batch, ADA_BLOCK), lambda l, j: (l, 0, j)),
        out_shape=jax.ShapeDtypeStruct((depth, batch, n), _F32),
        compiler_params=pltpu.CompilerParams(
            dimension_semantics=("arbitrary", "arbitrary"), vmem_limit_bytes=ADA_VMEM_LIMIT),
        name="ada_mod",
    )(c, w_ada, b_ada.reshape(depth, 1, n))


def _mixer_kernel(x_ref, mod_ref, w_in_ref, b_gate_ref, w_pool_ref, pool_scale_ref, bias_ref,
                  conv_w_ref, conv_b_ref, conv_g_ref, conv_beta_ref, w_bp_ref, w_ba_ref, w_bc_ref,
                  w_o_ref, ln_g_ref, ln_b_ref, o_ref,
                  u_buf, qe_buf, qo_buf, k_buf, v_buf, attn_buf, pool_buf, conv_buf, conv_shift,
                  score_buf, rowmax_buf, prob_buf, gate_buf, partial_buf, merged_buf):
    T, D = MIX_TILE, D_MODEL
    i = pl.program_id(1)

    @pl.when(i == 0)
    def _():
        k_buf[0:KV_HIST, :] = jnp.zeros((KV_HIST, D_ATTN), _BF16)
        for p in range(HEAD_PAIRS):
            v_buf[0:KV_HIST, 2 * p * LANES:(2 * p + 1) * LANES] = jnp.zeros((KV_HIST, LANES), _BF16)
            v_buf[:, (2 * p + 1) * LANES:2 * (p + 1) * LANES] = jnp.ones((KV_HIST + MIX_TILE, LANES), _BF16)
        pool_buf[0:POOL_HIST, :] = jnp.zeros((POOL_HIST, D_POOL), _F32)
        conv_buf[0:CONV_HIST, :] = jnp.zeros((CONV_HIST, D_CONV), _F32)

    sh_m = mod_ref[:, 0:D]
    sc_m = mod_ref[:, D:2 * D]
    g_m = mod_ref[:, 2 * D:3 * D]
    halves = (slice(0, T // 2), slice(T // 2, T))

    def in_proj(off, width, rows=slice(0, T)):
        return _dot(u_buf[rows, :], w_in_ref[:, off:off + width])

    lane = lax.broadcasted_iota(jnp.int32, (CHUNK, LANES), 1)
    key_idx = lax.broadcasted_iota(jnp.int32, (1, BAND), 1)
    dyn0 = jnp.minimum(i, 0)

    def scores(j):
        chunk = i * CHUNKS_PER_TILE + j
        first_valid = jnp.maximum(BAND_CHUNKS - 1 - chunk, 1) * CHUNK
        mask_add = jnp.where(key_idx >= first_valid, 0.0, NEG_INF)
        rows = slice(j * CHUNK, (j + 1) * CHUNK)
        band = slice(j * CHUNK, j * CHUNK + BAND)
        for p in range(HEAD_PAIRS):
            cs = slice(p * LANES, (p + 1) * LANES)
            slot = (j % 2) * HEAD_PAIRS + p + dyn0
            qs = jnp.concatenate([qe_buf[rows, cs], qo_buf[rows, cs]], axis=0)
            sc = lax.dot_general(qs, k_buf[band, cs], (((1,), (1,)), ((), ())),
                                 preferred_element_type=_F32) + bias_ref[p] + mask_add
            score_buf[slot] = sc
            rowmax_buf[slot] = jnp.max(sc, axis=-1, keepdims=True)

    def softmax(j):
        for p in range(HEAD_PAIRS):
            slot = (j % 2) * HEAD_PAIRS + p + dyn0
            prob_buf[slot] = jnp.exp2(score_buf[slot] - rowmax_buf[slot]).astype(_BF16)

    def weighted_values(j):
        rows = slice(j * CHUNK, (j + 1) * CHUNK)
        band = slice(j * CHUNK, j * CHUNK + BAND)
        for p in range(HEAD_PAIRS):
            slot = (j % 2) * HEAD_PAIRS + p + dyn0
            pv = _dot(prob_buf[slot], v_buf[band, 2 * p * LANES:2 * (p + 1) * LANES])
            pv = pv[:, 0:LANES] * (1.0 / pv[:, LANES:2 * LANES])
            attn_buf[rows, p * LANES:(p + 1) * LANES] = jnp.where(
                lane < HEAD_DIM, pv[0:CHUNK], pv[CHUNK:2 * CHUNK]).astype(_BF16)

    def gates(first, count):
        for g in range(first, first + count):
            cs = slice(g * COL_BLOCK, (g + 1) * COL_BLOCK)
            gate_buf[:, cs] = jax.nn.sigmoid(in_proj(OFF_GATE + g * COL_BLOCK, COL_BLOCK) + b_gate_ref[:, cs])

    even_head = (lax.broadcasted_iota(jnp.int32, (T // 2, D_ATTN), 1) & (LANES - 1)) < HEAD_DIM
    for rows in halves:
        u_buf[rows, :] = (_normalize(x_ref[rows, :]) * (1.0 + sc_m) + sh_m).astype(_BF16)
        q = in_proj(OFF_Q, D_ATTN, rows) * (HEAD_DIM ** -0.5 * LOG2_E)
        qe_buf[rows, :] = jnp.where(even_head, q, 0.0).astype(_BF16)
        qo_buf[rows, :] = jnp.where(even_head, 0.0, q).astype(_BF16)
        kv_rows = slice(KV_HIST + rows.start, KV_HIST + rows.stop)
        k_buf[kv_rows, :] = in_proj(OFF_K, D_ATTN, rows).astype(_BF16)
        v = in_proj(OFF_V, D_ATTN, rows).astype(_BF16)
        for p in range(HEAD_PAIRS):
            v_buf[kv_rows, 2 * p * LANES:(2 * p + 1) * LANES] = v[:, p * LANES:(p + 1) * LANES]
    scores(0)

    pool_buf[POOL_HIST:POOL_HIST + T, :] = in_proj(OFF_POOL, D_POOL)
    zc = in_proj(OFF_CONV, 2 * D_CONV)
    conv_buf[CONV_HIST:CONV_HIST + T, :] = zc[:, 0:D_CONV] * jax.nn.sigmoid(zc[:, D_CONV:2 * D_CONV])
    softmax(0)

    t_abs = i * T + lax.broadcasted_iota(jnp.int32, (T, LANES), 0)
    upper_half = lax.broadcasted_iota(jnp.int32, (T, LANES), 1) >= POOL_GROUP
    pooled = []
    for col, (w_lo, w_hi) in enumerate(((POOL_WINDOWS[0], POOL_WINDOWS[1]), (POOL_WINDOWS[2], POOL_WINDOWS[3]))):
        cs = slice(col * LANES, (col + 1) * LANES)
        x0 = pool_buf[POOL_HIST:POOL_HIST + T, cs]
        acc = x0
        for sh in range(1, w_hi):
            xs = pool_buf[POOL_HIST - sh:POOL_HIST - sh + T, cs]
            acc = acc + (xs if sh < w_lo else jnp.where(upper_half, xs, 0.0))
        window = jnp.where(upper_half, w_hi, w_lo)
        count = jnp.minimum(t_abs + 1, window).astype(_F32)
        pooled.append(acc / count - x0)
    pooled = jnp.concatenate(pooled, axis=1).astype(_BF16)
    pool_feat = (_dot(pooled, w_pool_ref[...]) * pool_scale_ref[...]).astype(_BF16)

    gate_blocks = N_BRANCH * D // COL_BLOCK
    for j in range(CHUNKS_PER_TILE):
        if j + 1 < CHUNKS_PER_TILE:
            scores(j + 1)
        weighted_values(j)
        first = j * gate_blocks // CHUNKS_PER_TILE
        gates(first, (j + 1) * gate_blocks // CHUNKS_PER_TILE - first)
        if j + 1 < CHUNKS_PER_TILE:
            softmax(j + 1)

    span = CONV_HIST + T - SUBLANES
    for r in range(1, SUBLANES):
        conv_shift[r - 1, 0:span, :] = conv_buf[r:r + span, :]
    acc = None
    for tap in range(CONV_WIDTH):
        start = CONV_HIST - (CONV_WIDTH - 1) + tap
        base, r = (start // SUBLANES) * SUBLANES, start % SUBLANES
        rows = conv_buf[base:base + T, :] if r == 0 else conv_shift[r - 1, base:base + T, :]
        term = conv_w_ref[tap:tap + 1, :] * rows
        acc = term if acc is None else acc + term
    hc = _normalize(acc + conv_b_ref[...]) * conv_g_ref[...] + conv_beta_ref[...]
    conv_feat = (hc * jax.nn.sigmoid(hc)).astype(_BF16)

    for cch in range(KV_HIST // CHUNK):
        dst = slice(cch * CHUNK, (cch + 1) * CHUNK)
        src = slice(T + cch * CHUNK, T + (cch + 1) * CHUNK)
        k_buf[dst, :] = k_buf[src, :]
        for p in range(HEAD_PAIRS):
            vs = slice(2 * p * LANES, (2 * p + 1) * LANES)
            v_buf[dst, vs] = v_buf[src, vs]
    pool_buf[0:POOL_HIST, :] = pool_buf[T:T + POOL_HIST, :]
    conv_buf[0:CONV_HIST, :] = conv_buf[T:T + CONV_HIST, :]

    attn_feat = attn_buf[...]
    col_blocks = D // COL_BLOCK
    for nb in range(col_blocks):
        cs = slice(nb * COL_BLOCK, (nb + 1) * COL_BLOCK)
        partial_buf[:, cs] = (gate_buf[:, cs] * _dot(pool_feat, w_bp_ref[:, cs])
                              + gate_buf[:, D + nb * COL_BLOCK:D + (nb + 1) * COL_BLOCK]
                              * _dot(attn_feat, w_ba_ref[:, cs]))
    for nb in range(col_blocks):
        cs = slice(nb * COL_BLOCK, (nb + 1) * COL_BLOCK)
        merged = (partial_buf[:, cs] + gate_buf[:, 2 * D + nb * COL_BLOCK:2 * D + (nb + 1) * COL_BLOCK]
                  * _dot(conv_feat, w_bc_ref[:, cs]))
        merged_buf[:, cs] = merged.astype(_BF16)

    token = None
    for rows in halves:
        gain = g_m if token is None else g_m + jnp.concatenate([token] * (D // LANES), axis=1)
        r = ALPHA * x_ref[rows, :] + gain * _dot(merged_buf[rows, :], w_o_ref[...])
        y = _normalize(r) * ln_g_ref[...] + ln_b_ref[...]
        o_ref[rows, :] = y
        token = _zero_token(y)


def _layer_spec(stacked, layer):
    index = (layer,) + (0,) * (stacked.ndim - 1)
    return pl.BlockSpec((None,) + stacked.shape[1:], lambda *_: index, pipeline_mode=pl.Buffered(1))


def _mixer_call(layer, x, mod, w_in, b_gate, w_pool_bd, pool_scale, bias_pairs, conv_w, conv_b,
                conv_g, conv_beta, w_bp, w_ba, w_bc, w_o, ln_g, ln_b):
    B, S, D = x.shape
    T = MIX_TILE
    consts = (w_in, b_gate, w_pool_bd, pool_scale, bias_pairs, conv_w, conv_b, conv_g, conv_beta,
              w_bp, w_ba, w_bc, w_o, ln_g, ln_b)
    return pl.pallas_call(
        _mixer_kernel,
        grid=(B, S // T),
        in_specs=[pl.BlockSpec((None, T, D), lambda b, i: (b, i, 0)),
                  pl.BlockSpec((None, None, 1, mod.shape[-1]), lambda b, i: (layer, b, 0, 0))]
                 + [_layer_spec(a, layer) for a in consts],
        out_specs=pl.BlockSpec((None, T, D), lambda b, i: (b, i, 0)),
        out_shape=jax.ShapeDtypeStruct((B, S, D), _F32),
        scratch_shapes=[
            pltpu.VMEM((T, D), _BF16),
            pltpu.VMEM((T, D_ATTN), _BF16),
            pltpu.VMEM((T, D_ATTN), _BF16),
            pltpu.VMEM((KV_HIST + T, D_ATTN), _BF16),
            pltpu.VMEM((KV_HIST + T, 2 * D_ATTN), _BF16),
            pltpu.VMEM((T, D_ATTN), _BF16),
            pltpu.VMEM((POOL_HIST + T, D_POOL), _F32),
            pltpu.VMEM((CONV_HIST + T, D_CONV), _F32),
            pltpu.VMEM((SUBLANES - 1, CONV_HIST + T, D_CONV), _F32),
            pltpu.VMEM((SCORE_SLOTS, 2 * CHUNK, BAND), _F32),
            pltpu.VMEM((SCORE_SLOTS, 2 * CHUNK, 1), _F32),
            pltpu.VMEM((SCORE_SLOTS, 2 * CHUNK, BAND), _BF16),
            pltpu.VMEM((T, N_BRANCH * D), _F32),
            pltpu.VMEM((T, D), _F32),
            pltpu.VMEM((T, D), _BF16),
        ],
        compiler_params=pltpu.CompilerParams(
            dimension_semantics=("arbitrary", "arbitrary"), vmem_limit_bytes=MIX_VMEM_LIMIT),
        name="token_mixer",
    )(x, mod, *consts)


def _ffn_kernel(x_ref, mod_ref, w1_ref, b1_ref, w2_ref, b2_ref, ln_g_ref, ln_b_ref, o_ref,
                u_even, u_odd, x_even, x_odd, r_even, r_odd, h_buf, *, n_tiles, tiles_per_seq):
    D = D_MODEL
    s = pl.program_id(0)
    n_blocks = D_FF // FF_BLOCK
    rows_per_slice = FFN_TILE // n_blocks

    def batch_of(tile):
        return lax.div(jnp.clip(tile, 0, n_tiles - 1), tiles_per_seq)

    def pre(rs, u_new, x_new):
        mod_new = mod_ref[batch_of(s)]
        x = x_ref[rs, :]
        x_new[rs, :] = x
        u = _normalize(x) * (1.0 + mod_new[:, 4 * D:5 * D]) + mod_new[:, 3 * D:4 * D]
        u_new[rs, :] = u.astype(_BF16)
        return u

    def post(rs, r_old):
        y = _normalize(r_old[rs, :]) * ln_g_ref[...] + ln_b_ref[...]
        o_ref[rs, :] = y
        return y

    def step(u_new, u_mid, x_new, x_mid, r_old, r_mid):
        mod_mid = mod_ref[batch_of(s - 1)]
        for c in range(n_blocks):
            rs = slice(c * rows_per_slice, (c + 1) * rows_per_slice)
            token = _zero_token(post(rs, r_old)) + _zero_token(pre(rs, u_new, x_new))
            cs = slice(c * FF_BLOCK, (c + 1) * FF_BLOCK)
            bias = b1_ref[:, cs] + jnp.concatenate([token] * (FF_BLOCK // LANES), axis=1)
            h = jnp.maximum(_dot(u_mid[...], w1_ref[:, cs]) + bias, 0.0)
            h_buf[:, cs] = (h * h).astype(_BF16)
        g_f = mod_mid[:, 5 * D:6 * D]
        for n in range(D // COL_BLOCK):
            ns = slice(n * COL_BLOCK, (n + 1) * COL_BLOCK)
            ff = _dot(h_buf[...], w2_ref[:, ns]) + b2_ref[:, ns]
            r_mid[:, ns] = ALPHA * x_mid[:, ns] + g_f[:, ns] * ff

    all_rows = slice(0, FFN_TILE)
    in_steady_state = jnp.logical_and(s >= 1, s <= n_tiles)
    parity = lax.rem(s, 2)
    r_last = r_odd if (n_tiles - 1) % 2 else r_even

    @pl.when(s == 0)
    def _():
        r_odd[...] = jnp.zeros(r_odd.shape, _F32)
        pre(all_rows, u_even, x_even)

    @pl.when(jnp.logical_and(in_steady_state, parity == 0))
    def _():
        step(u_even, u_odd, x_even, x_odd, r_even, r_odd)

    @pl.when(jnp.logical_and(in_steady_state, parity == 1))
    def _():
        step(u_odd, u_even, x_odd, x_even, r_odd, r_even)

    @pl.when(s == n_tiles + 1)
    def _():
        post(all_rows, r_last)


def _tile_spec(tile, n_tiles, lag):
    return pl.BlockSpec((tile, D_MODEL), lambda s: (jnp.clip(s - lag, 0, n_tiles - 1), 0))


def _ffn_call(layer, x, mod, w1, b1, w2, b2, ln_g, ln_b):
    B, S, D = x.shape
    T = FFN_TILE
    n_tiles = B * S // T
    consts = (w1, b1, w2, b2, ln_g, ln_b)
    kernel_fn = functools.partial(_ffn_kernel, n_tiles=n_tiles, tiles_per_seq=S // T)
    out = pl.pallas_call(
        kernel_fn,
        grid=(n_tiles + 2,),
        in_specs=[_tile_spec(T, n_tiles, 0),
                  pl.BlockSpec((None,) + mod.shape[1:], lambda s: (layer, 0, 0, 0))]
                 + [_layer_spec(a, layer) for a in consts],
        out_specs=_tile_spec(T, n_tiles, 2),
        out_shape=jax.ShapeDtypeStruct((B * S, D), _F32),
        scratch_shapes=[pltpu.VMEM((T, D), _BF16), pltpu.VMEM((T, D), _BF16),
                        pltpu.VMEM((T, D), _F32), pltpu.VMEM((T, D), _F32),
                        pltpu.VMEM((T, D), _F32), pltpu.VMEM((T, D), _F32),
                        pltpu.VMEM((T, D_FF), _BF16)],
        compiler_params=pltpu.CompilerParams(
            dimension_semantics=("arbitrary",), vmem_limit_bytes=FFN_VMEM_LIMIT),
        name="relu2_mlp",
    )(x.reshape(B * S, D), mod, *consts)
    return out.reshape(B, S, D)


def _bias_pairs(rel_bias):
    period = BAND + CHUNK - 1
    m = np.arange(period)
    ext_idx = np.clip(BAND - 1 - m, -REL_CLIP, REL_CLIP) + REL_CLIP
    ext = rel_bias[:, :, ext_idx] * LOG2_E
    lead = ext.shape[:-1]
    hankel = jnp.tile(ext, (1, 1, CHUNK + 1))[..., :CHUNK * (period + 1)]
    hankel = hankel.reshape(*lead, CHUNK, period + 1)[..., :BAND]
    bias = hankel[..., ::-1, :]
    return bias.reshape(lead[0], HEAD_PAIRS, 2 * CHUNK, BAND)


def _block_diag(w):
    depth, groups, gin, gout = w.shape
    eye = jnp.eye(groups, dtype=w.dtype)[None, :, None, :, None]
    return (w[:, :, :, None, :] * eye).reshape(depth, groups * gin, groups * gout)


def kernel(x, c, w_ada, b_ada, w_in, b_gate, w_pool, pool_scale, rel_bias, conv_w, conv_b, conv_ln_g, conv_ln_b, w_br_pool, w_br_attn, w_br_conv, w_o, ln_mix_g, ln_mix_b, w_ff1, b_ff1, w_ff2, b_ff2, ln_ff_g, ln_ff_b):
    depth = w_in.shape[0]
    bf16 = lambda a: a.astype(_BF16)
    rows = lambda a: a.reshape(depth, 1, -1)
    mod = _ada_call(c, w_ada, b_ada)[:, :, None, :]
    mixer_params = (bf16(w_in), rows(b_gate), bf16(_block_diag(w_pool)), rows(pool_scale),
                    _bias_pairs(rel_bias), conv_w, rows(conv_b), rows(conv_ln_g), rows(conv_ln_b),
                    bf16(w_br_pool), bf16(w_br_attn), bf16(w_br_conv), bf16(w_o),
                    rows(ln_mix_g), rows(ln_mix_b))
    ffn_params = (bf16(w_ff1), rows(b_ff1), bf16(w_ff2), rows(b_ff2), rows(ln_ff_g), rows(ln_ff_b))
    for layer in range(depth):
        x = _mixer_call(layer, x, mod, *mixer_params)
        x = _ffn_call(layer, x, mod, *ffn_params)
    return x
```

```python
import functools

import jax
import jax.numpy as jnp
import numpy as np
from jax import lax
from jax.experimental import pallas as pl
from jax.experimental.pallas import tpu as pltpu

D_MODEL = 1024
DEPTH = 2
CHUNK = 64
POOL_WINDOWS = (2, 4, 8, 16)
POOL_GROUP = 64
D_POOL = POOL_GROUP * len(POOL_WINDOWS)
N_HEADS = 8
HEAD_DIM = 64
D_ATTN = N_HEADS * HEAD_DIM
N_PREV_CHUNKS = 8
REL_CLIP = 128
D_CONV = 256
CONV_WIDTH = 31
D_FF = 4 * D_MODEL
N_BRANCH = 3
ALPHA = (2.0 * DEPTH) ** 0.25
LN_EPS = 1e-5
NEG_INF = -1e30
LOG2_E = 1.4426950408889634

OFF_POOL = 0
OFF_Q = OFF_POOL + D_POOL
OFF_K = OFF_Q + D_ATTN
OFF_V = OFF_K + D_ATTN
OFF_CONV = OFF_V + D_ATTN
OFF_GATE = OFF_CONV + 2 * D_CONV
D_IN = OFF_GATE + N_BRANCH * D_MODEL

LANES = 128
SUBLANES = 8
MIX_TILE = 512
CHUNKS_PER_TILE = MIX_TILE // CHUNK
BAND_CHUNKS = N_PREV_CHUNKS + 2
BAND = BAND_CHUNKS * CHUNK
KV_HIST = (BAND_CHUNKS - 1) * CHUNK
CONV_HIST = 32
POOL_HIST = 16
HEAD_PAIRS = N_HEADS // 2
SCORE_SLOTS = 2 * HEAD_PAIRS
COL_BLOCK = 256
FFN_TILE = 512
FF_BLOCK = 512
ADA_BLOCK = 1536
MIX_VMEM_LIMIT = 52 * 1024 * 1024
FFN_VMEM_LIMIT = 48 * 1024 * 1024
ADA_VMEM_LIMIT = 32 * 1024 * 1024

_F32 = jnp.float32
_BF16 = jnp.bfloat16


def _dot(a, b):
    return jnp.dot(a, b, preferred_element_type=_F32)


def _normalize(x):
    mu = jnp.mean(x, axis=-1, keepdims=True)
    xc = x - mu
    var = jnp.mean(xc * xc, axis=-1, keepdims=True)
    return xc * lax.rsqrt(var + LN_EPS)


def _zero_token(v):
    rows, cols = v.shape
    folded = v.reshape(rows // SUBLANES, SUBLANES, cols).sum(axis=0)
    folded = sum(folded[:, k * LANES:(k + 1) * LANES] for k in range(cols // LANES))
    bits = lax.bitcast_convert_type(folded, jnp.uint32)
    half = jnp.uint32(16)
    zero = lax.shift_right_logical(lax.shift_right_logical(bits, half), half)
    return lax.bitcast_convert_type(zero, _F32)[0:1, :]


def _ada_kernel(c_ref, w_ref, b_ref, o_ref):
    c = c_ref[...]
    ca = (c * jax.nn.sigmoid(c)).astype(_BF16)
    o_ref[...] = _dot(ca, w_ref[...].astype(_BF16)) + b_ref[...]


def _ada_call(c, w_ada, b_ada):
    depth, d, n = w_ada.shape
    batch = c.shape[0]
    return pl.pallas_call(
        _ada_kernel,
        grid=(depth, n // ADA_BLOCK),
        in_specs=[
            pl.BlockSpec((batch, d), lambda l, j: (0, 0)),
            pl.BlockSpec((None, d, ADA_BLOCK), lambda l, j: (l, 0, j)),
            pl.BlockSpec((None, 1, ADA_BLOCK), lambda l, j: (l, 0, j)),
        ],
        out_specs=pl.BlockSpec((None, batch, ADA_BLOCK), lambda l, j: (l, 0, j)),
        out_shape=jax.ShapeDtypeStruct((depth, batch, n), _F32),
        compiler_params=pltpu.CompilerParams(
            dimension_semantics=("arbitrary", "arbitrary"), vmem_limit_bytes=ADA_VMEM_LIMIT),
        name="ada_mod",
    )(c, w_ada, b_ada.reshape(depth, 1, n))


def _mixer_kernel(x_ref, mod_ref, w_in_ref, b_gate_ref, w_pool_ref, pool_scale_ref, bias_ref,
                  conv_w_ref, conv_b_ref, conv_g_ref, conv_beta_ref, w_bp_ref, w_ba_ref, w_bc_ref,
                  w_o_ref, ln_g_ref, ln_b_ref, o_ref,
                  u_buf, qe_buf, qo_buf, k_buf, v_buf, attn_buf, pool_buf, conv_buf, conv_shift,
                  score_buf, rowmax_buf, prob_buf, gate_buf, partial_buf, merged_buf):
    T, D = MIX_TILE, D_MODEL
    i = pl.program_id(1)

    @pl.when(i == 0)
    def _():
        k_buf[0:KV_HIST, :] = jnp.zeros((KV_HIST, D_ATTN), _BF16)
        for p in range(HEAD_PAIRS):
            v_buf[0:KV_HIST, 2 * p * LANES:(2 * p + 1) * LANES] = jnp.zeros((KV_HIST, LANES), _BF16)
            v_buf[:, (2 * p + 1) * LANES:2 * (p + 1) * LANES] = jnp.ones((KV_HIST + MIX_TILE, LANES), _BF16)
        pool_buf[0:POOL_HIST, :] = jnp.zeros((POOL_HIST, D_POOL), _F32)
        conv_buf[0:CONV_HIST, :] = jnp.zeros((CONV_HIST, D_CONV), _F32)

    sh_m = mod_ref[:, 0:D]
    sc_m = mod_ref[:, D:2 * D]
    g_m = mod_ref[:, 2 * D:3 * D]
    halves = (slice(0, T // 2), slice(T // 2, T))

    def in_proj(off, width, rows=slice(0, T)):
        return _dot(u_buf[rows, :], w_in_ref[:, off:off + width])

    lane = lax.broadcasted_iota(jnp.int32, (CHUNK, LANES), 1)
    key_idx = lax.broadcasted_iota(jnp.int32, (1, BAND), 1)
    dyn0 = jnp.minimum(i, 0)

    def scores(j):
        chunk = i * CHUNKS_PER_TILE + j
        first_valid = jnp.maximum(BAND_CHUNKS - 1 - chunk, 1) * CHUNK
        mask_add = jnp.where(key_idx >= first_valid, 0.0, NEG_INF)
        rows = slice(j * CHUNK, (j + 1) * CHUNK)
        band = slice(j * CHUNK, j * CHUNK + BAND)
        for p in range(HEAD_PAIRS):
            cs = slice(p * LANES, (p + 1) * LANES)
            slot = (j % 2) * HEAD_PAIRS + p + dyn0
            qs = jnp.concatenate([qe_buf[rows, cs], qo_buf[rows, cs]], axis=0)
            sc = lax.dot_general(qs, k_buf[band, cs], (((1,), (1,)), ((), ())),
                                 preferred_element_type=_F32) + bias_ref[p] + mask_add
            score_buf[slot] = sc
            rowmax_buf[slot] = jnp.max(sc, axis=-1, keepdims=True)

    def softmax(j):
        for p in range(HEAD_PAIRS):
            slot = (j % 2) * HEAD_PAIRS + p + dyn0
            prob_buf[slot] = jnp.exp2(score_buf[slot] - rowmax_buf[slot]).astype(_BF16)

    def weighted_values(j):
        rows = slice(j * CHUNK, (j + 1) * CHUNK)
        band = slice(j * CHUNK, j * CHUNK + BAND)
        for p in range(HEAD_PAIRS):
            slot = (j % 2) * HEAD_PAIRS + p + dyn0
            pv = _dot(prob_buf[slot], v_buf[band, 2 * p * LANES:2 * (p + 1) * LANES])
            pv = pv[:, 0:LANES] * (1.0 / pv[:, LANES:2 * LANES])
            attn_buf[rows, p * LANES:(p + 1) * LANES] = jnp.where(
                lane < HEAD_DIM, pv[0:CHUNK], pv[CHUNK:2 * CHUNK]).astype(_BF16)

    def gates(first, count):
        for g in range(first, first + count):
            cs = slice(g * COL_BLOCK, (g + 1) * COL_BLOCK)
            gate_buf[:, cs] = jax.nn.sigmoid(in_proj(OFF_GATE + g * COL_BLOCK, COL_BLOCK) + b_gate_ref[:, cs])

    even_head = (lax.broadcasted_iota(jnp.int32, (T // 2, D_ATTN), 1) & (LANES - 1)) < HEAD_DIM
    for rows in halves:
        u_buf[rows, :] = (_normalize(x_ref[rows, :]) * (1.0 + sc_m) + sh_m).astype(_BF16)
        q = in_proj(OFF_Q, D_ATTN, rows) * (HEAD_DIM ** -0.5 * LOG2_E)
        qe_buf[rows, :] = jnp.where(even_head, q, 0.0).astype(_BF16)
        qo_buf[rows, :] = jnp.where(even_head, 0.0, q).astype(_BF16)
        kv_rows = slice(KV_HIST + rows.start, KV_HIST + rows.stop)
        k_buf[kv_rows, :] = in_proj(OFF_K, D_ATTN, rows).astype(_BF16)
        v = in_proj(OFF_V, D_ATTN, rows).astype(_BF16)
        for p in range(HEAD_PAIRS):
            v_buf[kv_rows, 2 * p * LANES:(2 * p + 1) * LANES] = v[:, p * LANES:(p + 1) * LANES]
    scores(0)

    pool_buf[POOL_HIST:POOL_HIST + T, :] = in_proj(OFF_POOL, D_POOL)
    zc = in_proj(OFF_CONV, 2 * D_CONV)
    conv_buf[CONV_HIST:CONV_HIST + T, :] = zc[:, 0:D_CONV] * jax.nn.sigmoid(zc[:, D_CONV:2 * D_CONV])
    softmax(0)

    t_abs = i * T + lax.broadcasted_iota(jnp.int32, (T, LANES), 0)
    upper_half = lax.broadcasted_iota(jnp.int32, (T, LANES), 1) >= POOL_GROUP
    pooled = []
    for col, (w_lo, w_hi) in enumerate(((POOL_WINDOWS[0], POOL_WINDOWS[1]), (POOL_WINDOWS[2], POOL_WINDOWS[3]))):
        cs = slice(col * LANES, (col + 1) * LANES)
        x0 = pool_buf[POOL_HIST:POOL_HIST + T, cs]
        acc = x0
        for sh in range(1, w_hi):
            xs = pool_buf[POOL_HIST - sh:POOL_HIST - sh + T, cs]
            acc = acc + (xs if sh < w_lo else jnp.where(upper_half, xs, 0.0))
        window = jnp.where(upper_half, w_hi, w_lo)
        count = jnp.minimum(t_abs + 1, window).astype(_F32)
        pooled.append(acc / count - x0)
    pooled = jnp.concatenate(pooled, axis=1).astype(_BF16)
    pool_feat = (_dot(pooled, w_pool_ref[...]) * pool_scale_ref[...]).astype(_BF16)

    gate_blocks = N_BRANCH * D // COL_BLOCK
    for j in range(CHUNKS_PER_TILE):
        if j + 1 < CHUNKS_PER_TILE:
            scores(j + 1)
        weighted_values(j)
        first = j * gate_blocks // CHUNKS_PER_TILE
        gates(first, (j + 1) * gate_blocks // CHUNKS_PER_TILE - first)
        if j + 1 < CHUNKS_PER_TILE:
            softmax(j + 1)

    span = CONV_HIST + T - SUBLANES
    for r in range(1, SUBLANES):
        conv_shift[r - 1, 0:span, :] = conv_buf[r:r + span, :]
    acc = None
    for tap in range(CONV_WIDTH):
        start = CONV_HIST - (CONV_WIDTH - 1) + tap
        base, r = (start // SUBLANES) * SUBLANES, start % SUBLANES
        rows = conv_buf[base:base + T, :] if r == 0 else conv_shift[r - 1, base:base + T, :]
        term = conv_w_ref[tap:tap + 1, :] * rows
        acc = term if acc is None else acc + term
    hc = _normalize(acc + conv_b_ref[...]) * conv_g_ref[...] + conv_beta_ref[...]
    conv_feat = (hc * jax.nn.sigmoid(hc)).astype(_BF16)

    for cch in range(KV_HIST // CHUNK):
        dst = slice(cch * CHUNK, (cch + 1) * CHUNK)
        src = slice(T + cch * CHUNK, T + (cch + 1) * CHUNK)
        k_buf[dst, :] = k_buf[src, :]
        for p in range(HEAD_PAIRS):
            vs = slice(2 * p * LANES, (2 * p + 1) * LANES)
            v_buf[dst, vs] = v_buf[src, vs]
    pool_buf[0:POOL_HIST, :] = pool_buf[T:T + POOL_HIST, :]
    conv_buf[0:CONV_HIST, :] = conv_buf[T:T + CONV_HIST, :]

    attn_feat = attn_buf[...]
    col_blocks = D // COL_BLOCK
    for nb in range(col_blocks):
        cs = slice(nb * COL_BLOCK, (nb + 1) * COL_BLOCK)
        partial_buf[:, cs] = (gate_buf[:, cs] * _dot(pool_feat, w_bp_ref[:, cs])
                              + gate_buf[:, D + nb * COL_BLOCK:D + (nb + 1) * COL_BLOCK]
                              * _dot(attn_feat, w_ba_ref[:, cs]))
    for nb in range(col_blocks):
        cs = slice(nb * COL_BLOCK, (nb + 1) * COL_BLOCK)
        merged = (partial_buf[:, cs] + gate_buf[:, 2 * D + nb * COL_BLOCK:2 * D + (nb + 1) * COL_BLOCK]
                  * _dot(conv_feat, w_bc_ref[:, cs]))
        merged_buf[:, cs] = merged.astype(_BF16)

    token = None
    for rows in halves:
        gain = g_m if token is None else g_m + jnp.concatenate([token] * (D // LANES), axis=1)
        r = ALPHA * x_ref[rows, :] + gain * _dot(merged_buf[rows, :], w_o_ref[...])
        y = _normalize(r) * ln_g_ref[...] + ln_b_ref[...]
        o_ref[rows, :] = y
        token = _zero_token(y)


def _layer_spec(stacked, layer):
    index = (layer,) + (0,) * (stacked.ndim - 1)
    return pl.BlockSpec((None,) + stacked.shape[1:], lambda *_: index, pipeline_mode=pl.Buffered(1))


def _mixer_call(layer, x, mod, w_in, b_gate, w_pool_bd, pool_scale, bias_pairs, conv_w, conv_b,
                conv_g, conv_beta, w_bp, w_ba, w_bc, w_o, ln_g, ln_b):
    B, S, D = x.shape
    T = MIX_TILE
    consts = (w_in, b_gate, w_pool_bd, pool_scale, bias_pairs, conv_w, conv_b, conv_g, conv_beta,
              w_bp, w_ba, w_bc, w_o, ln_g, ln_b)
    return pl.pallas_call(
        _mixer_kernel,
        grid=(B, S // T),
        in_specs=[pl.BlockSpec((None, T, D), lambda b, i: (b, i, 0)),
                  pl.BlockSpec((None, None, 1, mod.shape[-1]), lambda b, i: (layer, b, 0, 0))]
                 + [_layer_spec(a, layer) for a in consts],
        out_specs=pl.BlockSpec((None, T, D), lambda b, i: (b, i, 0)),
        out_shape=jax.ShapeDtypeStruct((B, S, D), _F32),
        scratch_shapes=[
            pltpu.VMEM((T, D), _BF16),
            pltpu.VMEM((T, D_ATTN), _BF16),
            pltpu.VMEM((T, D_ATTN), _BF16),
            pltpu.VMEM((KV_HIST + T, D_ATTN), _BF16),
            pltpu.VMEM((KV_HIST + T, 2 * D_ATTN), _BF16),
            pltpu.VMEM((T, D_ATTN), _BF16),
            pltpu.VMEM((POOL_HIST + T, D_POOL), _F32),
            pltpu.VMEM((CONV_HIST + T, D_CONV), _F32),
            pltpu.VMEM((SUBLANES - 1, CONV_HIST + T, D_CONV), _F32),
            pltpu.VMEM((SCORE_SLOTS, 2 * CHUNK, BAND), _F32),
            pltpu.VMEM((SCORE_SLOTS, 2 * CHUNK, 1), _F32),
            pltpu.VMEM((SCORE_SLOTS, 2 * CHUNK, BAND), _BF16),
            pltpu.VMEM((T, N_BRANCH * D), _F32),
            pltpu.VMEM((T, D), _F32),
            pltpu.VMEM((T, D), _BF16),
        ],
        compiler_params=pltpu.CompilerParams(
            dimension_semantics=("arbitrary", "arbitrary"), vmem_limit_bytes=MIX_VMEM_LIMIT,
            allow_input_fusion=[False, False] + [a.dtype == _BF16 for a in consts]),
        name="token_mixer",
    )(x, mod, *consts)


def _ffn_kernel(x_ref, mod_ref, w1_ref, b1_ref, w2_ref, b2_ref, ln_g_ref, ln_b_ref, o_ref,
                u_even, u_odd, x_even, x_odd, r_even, r_odd, h_buf, *, n_tiles, tiles_per_seq):
    D = D_MODEL
    s = pl.program_id(0)
    n_blocks = D_FF // FF_BLOCK
    rows_per_slice = FFN_TILE // n_blocks

    def batch_of(tile):
        return lax.div(jnp.clip(tile, 0, n_tiles - 1), tiles_per_seq)

    def pre(rs, u_new, x_new):
        mod_new = mod_ref[batch_of(s)]
        x = x_ref[rs, :]
        x_new[rs, :] = x
        u = _normalize(x) * (1.0 + mod_new[:, 4 * D:5 * D]) + mod_new[:, 3 * D:4 * D]
        u_new[rs, :] = u.astype(_BF16)
        return u

    def post(rs, r_old):
        y = _normalize(r_old[rs, :]) * ln_g_ref[...] + ln_b_ref[...]
        o_ref[rs, :] = y
        return y

    def step(u_new, u_mid, x_new, x_mid, r_old, r_mid):
        mod_mid = mod_ref[batch_of(s - 1)]
        for c in range(n_blocks):
            rs = slice(c * rows_per_slice, (c + 1) * rows_per_slice)
            token = _zero_token(post(rs, r_old)) + _zero_token(pre(rs, u_new, x_new))
            cs = slice(c * FF_BLOCK, (c + 1) * FF_BLOCK)
            bias = b1_ref[:, cs] + jnp.concatenate([token] * (FF_BLOCK // LANES), axis=1)
            h = jnp.maximum(_dot(u_mid[...], w1_ref[:, cs]) + bias, 0.0)
            h_buf[:, cs] = (h * h).astype(_BF16)
        g_f = mod_mid[:, 5 * D:6 * D]
        for n in range(D // COL_BLOCK):
            ns = slice(n * COL_BLOCK, (n + 1) * COL_BLOCK)
            ff = _dot(h_buf[...], w2_ref[:, ns]) + b2_ref[:, ns]
            r_mid[:, ns] = ALPHA * x_mid[:, ns] + g_f[:, ns] * ff

    all_rows = slice(0, FFN_TILE)
    in_steady_state = jnp.logical_and(s >= 1, s <= n_tiles)
    parity = lax.rem(s, 2)
    r_last = r_odd if (n_tiles - 1) % 2 else r_even

    @pl.when(s == 0)
    def _():
        r_odd[...] = jnp.zeros(r_odd.shape, _F32)
        pre(all_rows, u_even, x_even)

    @pl.when(jnp.logical_and(in_steady_state, parity == 0))
    def _():
        step(u_even, u_odd, x_even, x_odd, r_even, r_odd)

    @pl.when(jnp.logical_and(in_steady_state, parity == 1))
    def _():
        step(u_odd, u_even, x_odd, x_even, r_odd, r_even)

    @pl.when(s == n_tiles + 1)
    def _():
        post(all_rows, r_last)


def _tile_spec(tile, n_tiles, lag):
    return pl.BlockSpec((tile, D_MODEL), lambda s: (jnp.clip(s - lag, 0, n_tiles - 1), 0))


def _ffn_call(layer, x, mod, w1, b1, w2, b2, ln_g, ln_b):
    B, S, D = x.shape
    T = FFN_TILE
    n_tiles = B * S // T
    consts = (w1, b1, w2, b2, ln_g, ln_b)
    kernel_fn = functools.partial(_ffn_kernel, n_tiles=n_tiles, tiles_per_seq=S // T)
    out = pl.pallas_call(
        kernel_fn,
        grid=(n_tiles + 2,),
        in_specs=[_tile_spec(T, n_tiles, 0),
                  pl.BlockSpec((None,) + mod.shape[1:], lambda s: (layer, 0, 0, 0))]
                 + [_layer_spec(a, layer) for a in consts],
        out_specs=_tile_spec(T, n_tiles, 2),
        out_shape=jax.ShapeDtypeStruct((B * S, D), _F32),
        scratch_shapes=[pltpu.VMEM((T, D), _BF16), pltpu.VMEM((T, D), _BF16),
                        pltpu.VMEM((T, D), _F32), pltpu.VMEM((T, D), _F32),
                        pltpu.VMEM((T, D), _F32), pltpu.VMEM((T, D), _F32),
                        pltpu.VMEM((T, D_FF), _BF16)],
        compiler_params=pltpu.CompilerParams(
            dimension_semantics=("arbitrary",), vmem_limit_bytes=FFN_VMEM_LIMIT,
            allow_input_fusion=[False, False, True, False, True, False, False, False]),
        name="relu2_mlp",
    )(x.reshape(B * S, D), mod, *consts)
    return out.reshape(B, S, D)


def _bias_pairs(rel_bias):
    period = BAND + CHUNK - 1
    m = np.arange(period)
    ext_idx = np.clip(BAND - 1 - m, -REL_CLIP, REL_CLIP) + REL_CLIP
    ext = rel_bias[:, :, ext_idx] * LOG2_E
    lead = ext.shape[:-1]
    hankel = jnp.tile(ext, (1, 1, CHUNK + 1))[..., :CHUNK * (period + 1)]
    hankel = hankel.reshape(*lead, CHUNK, period + 1)[..., :BAND]
    bias = hankel[..., ::-1, :]
    return bias.reshape(lead[0], HEAD_PAIRS, 2 * CHUNK, BAND)


def _block_diag(w):
    depth, groups, gin, gout = w.shape
    eye = jnp.eye(groups, dtype=w.dtype)[None, :, None, :, None]
    return (w[:, :, :, None, :] * eye).reshape(depth, groups * gin, groups * gout)


def kernel(x, c, w_ada, b_ada, w_in, b_gate, w_pool, pool_scale, rel_bias, conv_w, conv_b, conv_ln_g, conv_ln_b, w_br_pool, w_br_attn, w_br_conv, w_o, ln_mix_g, ln_mix_b, w_ff1, b_ff1, w_ff2, b_ff2, ln_ff_g, ln_ff_b):
    depth = w_in.shape[0]
    bf16 = lambda a: a.astype(_BF16)
    rows = lambda a: a.reshape(depth, 1, -1)
    mod = _ada_call(c, w_ada, b_ada)[:, :, None, :]
    mixer_params = (bf16(w_in), rows(b_gate), bf16(_block_diag(w_pool)), rows(pool_scale),
                    _bias_pairs(rel_bias), conv_w, rows(conv_b), rows(conv_ln_g), rows(conv_ln_b),
                    bf16(w_br_pool), bf16(w_br_attn), bf16(w_br_conv), bf16(w_o),
                    rows(ln_mix_g), rows(ln_mix_b))
    ffn_params = (bf16(w_ff1), rows(b_ff1), bf16(w_ff2), rows(b_ff2), rows(ln_ff_g), rows(ln_ff_b))
    for layer in range(depth):
        x = _mixer_call(layer, x, mod, *mixer_params)
        x = _ffn_call(layer, x, mod, *ffn_params)
    return x
```
